```python
import jax, jax.numpy as jnp
from jax import lax
import numpy as np

D_MODEL = 1024
BATCH = 8
SEQ = 4096
DEPTH = 2

LRU_WIDTH = D_MODEL
LRU_HEADS = 8
LRU_HEAD_DIM = LRU_WIDTH // LRU_HEADS
LRU_C = 8.0
CONV_WIDTH = 4
DN_HEADS = 8
DN_HEAD_K = 128
DN_HEAD_V = 128
DN_KEY_DIM = DN_HEADS * DN_HEAD_K
DN_VALUE_DIM = DN_HEADS * DN_HEAD_V
DN_CONV_DIM = 2 * DN_KEY_DIM + DN_VALUE_DIM
CHUNK = 64
D_MIX = LRU_WIDTH + DN_VALUE_DIM
D_IN = 2 * LRU_WIDTH + DN_CONV_DIM + DN_VALUE_DIM + 2 * DN_HEADS
SPLITS = [LRU_WIDTH, 2 * LRU_WIDTH, 2 * LRU_WIDTH + DN_CONV_DIM,
          2 * LRU_WIDTH + DN_CONV_DIM + DN_VALUE_DIM,
          2 * LRU_WIDTH + DN_CONV_DIM + DN_VALUE_DIM + DN_HEADS]
EPS = 1e-6

kernel_name = "hymba_rglru_gated_deltanet_trunk"


def rmsnorm(x, w):
    xf = x.astype(jnp.float32)
    y = xf * lax.rsqrt(jnp.mean(xf * xf, axis=-1, keepdims=True) + EPS)
    return (y * w.astype(jnp.float32)).astype(x.dtype)


def gated_rmsnorm(o, z, w):
    of = o.astype(jnp.float32)
    y = of * lax.rsqrt(jnp.mean(of * of, axis=-1, keepdims=True) + EPS)
    return (y * w.astype(jnp.float32) * jax.nn.silu(z.astype(jnp.float32))).astype(o.dtype)


def l2norm(x):
    return x * lax.rsqrt(jnp.sum(x * x, axis=-1, keepdims=True) + EPS)


def causal_dwconv(x, w):
    K = w.shape[0]
    S = x.shape[1]
    xp = jnp.pad(x, ((0, 0), (K - 1, 0), (0, 0)))
    y = xp[:, 0:S] * w[0]
    for j in range(1, K):
        y = y + xp[:, j:j + S] * w[j]
    return y


def rg_lru(x, wa, ba, wx, bx, lam):
    B_, S_, W = x.shape
    xh = x.reshape(B_, S_, LRU_HEADS, LRU_HEAD_DIM)
    r = jax.nn.sigmoid(jnp.einsum('bshi,hij->bshj', xh, wa).reshape(B_, S_, W) + ba)
    i = jax.nn.sigmoid(jnp.einsum('bshi,hij->bshj', xh, wx).reshape(B_, S_, W) + bx)
    log_a = -LRU_C * r * jax.nn.softplus(-lam)
    a = jnp.exp(log_a)
    mult = jnp.sqrt(-jnp.expm1(2.0 * log_a))
    b = mult * (i * x)

    def combine(left, right):
        a_l, b_l = left
        a_r, b_r = right
        return a_l * a_r, a_r * b_l + b_r

    _, h = lax.associative_scan(combine, (a, b), axis=1)
    return h


def gated_delta_rule(q, k, v, g, beta):
    B_, S_, H, dk = q.shape
    dv = v.shape[-1]
    N = S_ // CHUNK

    def to_chunks(t):
        return t.reshape(B_, N, CHUNK, H, t.shape[-1]).transpose(0, 3, 1, 2, 4)

    q, k, v = to_chunks(q), to_chunks(k), to_chunks(v)
    g = g.reshape(B_, N, CHUNK, H).transpose(0, 3, 1, 2)
    beta = beta.reshape(B_, N, CHUNK, H).transpose(0, 3, 1, 2)
    k_beta = k * beta[..., None]
    v_beta = v * beta[..., None]
    gc = jnp.cumsum(g, axis=-1)
    tri = jnp.tril(jnp.ones((CHUNK, CHUNK), dtype=bool))
    strict = jnp.tril(jnp.ones((CHUNK, CHUNK), dtype=bool), -1)
    decay_mask = jnp.exp(jnp.where(tri, gc[..., :, None] - gc[..., None, :], -jnp.inf))
    A = jnp.where(strict, jnp.einsum('bhncd,bhnjd->bhncj', k_beta, k) * decay_mask, 0.0)
    T = A + jnp.eye(CHUNK, dtype=A.dtype)
    u = lax.linalg.triangular_solve(T, v_beta, left_side=True, lower=True, unit_diagonal=True)
    w = lax.linalg.triangular_solve(T, k_beta * jnp.exp(gc)[..., None],
                                    left_side=True, lower=True, unit_diagonal=True)
    attn = jnp.where(tri, jnp.einsum('bhncd,bhnjd->bhncj', q, k) * decay_mask, 0.0)

    def step(state, xs):
        q_i, k_i, u_i, w_i, attn_i, gc_i = xs
        v_new = u_i - jnp.einsum('bhcd,bhde->bhce', w_i, state)
        o_i = (jnp.einsum('bhcd,bhde->bhce', q_i * jnp.exp(gc_i)[..., None], state)
               + jnp.einsum('bhcj,bhje->bhce', attn_i, v_new))
        g_last = gc_i[..., -1]
        state = (state * jnp.exp(g_last)[..., None, None]
                 + jnp.einsum('bhcd,bhce->bhde', k_i * jnp.exp(g_last[..., None] - gc_i)[..., None], v_new))
        return state, o_i

    xs = tuple(jnp.moveaxis(t, 2, 0) for t in (q, k, u, w, attn, gc))
    state0 = jnp.zeros((B_, H, dk, dv), dtype=q.dtype)
    _, o = lax.scan(step, state0, xs)
    return o.transpose(1, 0, 3, 2, 4).reshape(B_, S_, H, dv)


def hybrid_layer(x, norm_w, w_in, lru_conv_w, lru_conv_b, lru_wa, lru_ba, lru_wx, lru_bx,
                 lru_lambda, lru_norm_w, dn_conv_w, dn_A_log, dn_dt_bias, dn_norm_w, w_out):
    B_, S_, _ = x.shape
    f32 = jnp.float32
    h = rmsnorm(x, norm_w)
    proj = jnp.einsum('bsd,de->bse', h, w_in)
    lru_x, lru_z, dn_qkv, dn_z, dn_b, dn_a = jnp.split(proj, SPLITS, axis=-1)

    xc = (causal_dwconv(lru_x, lru_conv_w) + lru_conv_b).astype(f32)
    hl = rg_lru(xc, lru_wa.astype(f32), lru_ba.astype(f32), lru_wx.astype(f32),
                lru_bx.astype(f32), lru_lambda.astype(f32))
    y_lru = gated_rmsnorm(hl.reshape(B_, S_, LRU_HEADS, LRU_HEAD_DIM),
                          lru_z.reshape(B_, S_, LRU_HEADS, LRU_HEAD_DIM),
                          lru_norm_w.reshape(LRU_HEADS, LRU_HEAD_DIM))

    qkv = jax.nn.silu(causal_dwconv(dn_qkv, dn_conv_w)).astype(f32)
    q, k, v = jnp.split(qkv, [DN_KEY_DIM, 2 * DN_KEY_DIM], axis=-1)
    q = l2norm(q.reshape(B_, S_, DN_HEADS, DN_HEAD_K)) * (DN_HEAD_K ** -0.5)
    k = l2norm(k.reshape(B_, S_, DN_HEADS, DN_HEAD_K))
    v = v.reshape(B_, S_, DN_HEADS, DN_HEAD_V)
    beta = jax.nn.sigmoid(dn_b.astype(f32))
    g = -jnp.exp(dn_A_log.astype(f32)) * jax.nn.softplus(dn_a.astype(f32) + dn_dt_bias.astype(f32))
    o = gated_delta_rule(q, k, v, g, beta)
    y_dn = gated_rmsnorm(o, dn_z.reshape(B_, S_, DN_HEADS, DN_HEAD_V), dn_norm_w)

    y = jnp.concatenate([y_lru.reshape(B_, S_, LRU_WIDTH), y_dn.reshape(B_, S_, DN_VALUE_DIM)],
                        axis=-1).astype(x.dtype)
    return x + jnp.einsum('bse,ed->bsd', y, w_out).astype(x.dtype)


def setup_inputs(seed: int = 0) -> dict:
    key = jax.random.key(seed)
    ks = jax.random.split(key, 20)
    nrm = jax.random.normal
    L = DEPTH
    x = nrm(ks[0], (BATCH, SEQ, D_MODEL), jnp.float32)
    norm_w = 1.0 + 0.01 * nrm(ks[1], (L, D_MODEL), jnp.float32)
    w_in = nrm(ks[2], (L, D_MODEL, D_IN), jnp.float32) * D_MODEL ** -0.5
    lru_conv_w = nrm(ks[3], (L, CONV_WIDTH, LRU_WIDTH), jnp.float32) * CONV_WIDTH ** -0.5
    lru_conv_b = 0.01 * nrm(ks[4], (L, LRU_WIDTH), jnp.float32)
    lru_wa = nrm(ks[5], (L, LRU_HEADS, LRU_HEAD_DIM, LRU_HEAD_DIM), jnp.float32) * LRU_HEAD_DIM ** -0.5
    lru_ba = 0.01 * nrm(ks[6], (L, LRU_WIDTH), jnp.float32)
    lru_wx = nrm(ks[7], (L, LRU_HEADS, LRU_HEAD_DIM, LRU_HEAD_DIM), jnp.float32) * LRU_HEAD_DIM ** -0.5
    lru_bx = 0.01 * nrm(ks[8], (L, LRU_WIDTH), jnp.float32)
    a0 = jax.random.uniform(ks[9], (L, LRU_WIDTH), jnp.float32, 0.9, 0.999)
    s = a0 ** (1.0 / LRU_C)
    lru_lambda = jnp.log(s) - jnp.log1p(-s)
    lru_norm_w = 1.0 + 0.01 * nrm(ks[10], (L, LRU_WIDTH), jnp.float32)
    dn_conv_w = nrm(ks[11], (L, CONV_WIDTH, DN_CONV_DIM), jnp.float32) * CONV_WIDTH ** -0.5
    dn_A_log = jnp.log(jax.random.uniform(ks[12], (L, DN_HEADS), jnp.float32, 1.0, 16.0))
    dt = jnp.exp(jax.random.uniform(ks[13], (L, DN_HEADS), jnp.float32,
                                    float(np.log(1e-3)), float(np.log(1e-1))))
    dn_dt_bias = dt + jnp.log(-jnp.expm1(-dt))
    dn_norm_w = 1.0 + 0.01 * nrm(ks[14], (L, DN_HEAD_V), jnp.float32)
    w_out = nrm(ks[15], (L, D_MIX, D_MODEL), jnp.float32) * D_MIX ** -0.5
    final_norm_w = 1.0 + 0.01 * nrm(ks[16], (D_MODEL,), jnp.float32)
    return {"x": x, "norm_w": norm_w, "w_in": w_in, "lru_conv_w": lru_conv_w,
            "lru_conv_b": lru_conv_b, "lru_wa": lru_wa, "lru_ba": lru_ba, "lru_wx": lru_wx,
            "lru_bx": lru_bx, "lru_lambda": lru_lambda, "lru_norm_w": lru_norm_w,
            "dn_conv_w": dn_conv_w, "dn_A_log": dn_A_log, "dn_dt_bias": dn_dt_bias,
            "dn_norm_w": dn_norm_w, "w_out": w_out, "final_norm_w": final_norm_w}


def reference(x, norm_w, w_in, lru_conv_w, lru_conv_b, lru_wa, lru_ba, lru_wx, lru_bx,
              lru_lambda, lru_norm_w, dn_conv_w, dn_A_log, dn_dt_bias, dn_norm_w, w_out,
              final_norm_w):
    h = x
    for l in range(DEPTH):
        h = hybrid_layer(h, norm_w[l], w_in[l], lru_conv_w[l], lru_conv_b[l], lru_wa[l], lru_ba[l],
                         lru_wx[l], lru_bx[l], lru_lambda[l], lru_norm_w[l], dn_conv_w[l],
                         dn_A_log[l], dn_dt_bias[l], dn_norm_w[l], w_out[l])
    return rmsnorm(h, final_norm_w)
```

```python
import functools

import jax
import jax.numpy as jnp
from jax import lax
from jax.experimental import pallas as pl
from jax.experimental.pallas import tpu as pltpu

F32 = jnp.float32
BF16 = jnp.bfloat16

N_HEADS = 8
HEAD_DIM = 128
CONV_WIDTH = 4
LRU_C = 8.0
CHUNK = 64
EPS = 1e-6
SUBLANES = 8
LANES = 128
N_CONV_COLS = 4 * HEAD_DIM
N_HEAD_COLS = 6 * HEAD_DIM
CARRY_ROWS = (CONV_WIDTH - 1) * SUBLANES
VMEM_LIMIT_BYTES = 56 * 1024 * 1024


_sigmoid = jax.nn.sigmoid
_silu = jax.nn.silu
_softplus = jax.nn.softplus


def _dot(a, b):
    return jnp.dot(a, b, preferred_element_type=F32)


def _dot_nt(a, b):
    return lax.dot_general(a, b, (((1,), (1,)), ((), ())), preferred_element_type=F32)


def _dot_tn(a, b):
    return lax.dot_general(a, b, (((0,), (0,)), ((), ())), preferred_element_type=F32)


def _rms(x, w):
    return x * lax.rsqrt(jnp.mean(x * x, axis=-1, keepdims=True) + EPS) * w


def _layer_kernel(x_ref, nw_ref, wh_ref, wba_ref, cw_ref, cb_ref, wg_ref, bg_ref, lam_ref,
                  lnw_ref, dnw_ref, alog_ref, dt_ref, wo_ref, fnw_ref,
                  o_ref,
                  xn_s, bg3_s, p_s, carry_s, hl_s, hlru_s, q_s, k_s, kb_s, kbg_s, qg_s, kd_s,
                  vb_s, gcb_s, od_s, st_s, *, final):
    rows = CHUNK * SUBLANES
    step = pl.program_id(0)

    @pl.when(step == 0)
    def _():
        carry_s[...] = jnp.zeros_like(carry_s)
        hlru_s[...] = jnp.zeros_like(hlru_s)
        st_s[...] = jnp.zeros_like(st_s)

    x = x_ref[...]
    xn_s[...] = _rms(x, nw_ref[...]).astype(BF16)
    o_ref[...] = x

    ba = _dot(xn_s[...], wba_ref[...])
    beta = _sigmoid(ba)
    g = -jnp.exp(alog_ref[...]) * _softplus(ba + dt_ref[...])
    acc = jnp.zeros((SUBLANES, LANES), F32)
    pieces = []
    for s in range(CHUNK):
        acc = acc + g[s * SUBLANES:(s + 1) * SUBLANES]
        pieces.append(acc)
    gc = jnp.concatenate(pieces, axis=0)
    lane = lax.broadcasted_iota(jnp.int32, (rows, LANES), 1)
    bgv = jnp.where(lane < N_HEADS, beta, gc)
    hi = bgv.astype(BF16)
    r1 = bgv - hi.astype(F32)
    mid = r1.astype(BF16)
    lo = (r1 - mid.astype(F32)).astype(BF16)
    bg3_s[...] = jnp.concatenate([hi, mid, lo], axis=1)

    ii = lax.broadcasted_iota(jnp.int32, (CHUNK, CHUNK), 0)
    jj = lax.broadcasted_iota(jnp.int32, (CHUNK, CHUNK), 1)
    tri = ii >= jj
    strict = ii > jj
    erow = lax.broadcasted_iota(jnp.int32, (LANES, 2 * LANES), 0)
    ecol = lax.broadcasted_iota(jnp.int32, (LANES, 2 * LANES), 1)

    def head_body(h, carry):
        pf = _dot(xn_s[...], wh_ref[h])
        pc = pf[:, :N_CONV_COLS]
        p_s[0:CARRY_ROWS, :] = carry_s[h]
        p_s[CARRY_ROWS:CARRY_ROWS + rows, :] = pc
        carry_s[h] = pc[rows - CARRY_ROWS:rows, :]
        cwh = cw_ref[h]
        y = cwh[CONV_WIDTH - 1:CONV_WIDTH] * pc
        for j in range(CONV_WIDTH - 1):
            y = y + cwh[j:j + 1] * p_s[j * SUBLANES:j * SUBLANES + rows, :]
        xc = y[:, :HEAD_DIM] + cb_ref[h]
        qkv = _silu(y[:, HEAD_DIM:])
        z_lru = pf[:, N_CONV_COLS:N_CONV_COLS + HEAD_DIM]
        z_dn = pf[:, N_CONV_COLS + HEAD_DIM:]

        gates = _dot(xc.astype(BF16), wg_ref[h]) + bg_ref[h]
        r = _sigmoid(gates[:, :HEAD_DIM])
        ig = _sigmoid(gates[:, HEAD_DIM:])
        log_a = r * (-LRU_C * _softplus(-lam_ref[h]))
        a = jnp.exp(log_a)
        mult = jnp.sqrt(jnp.tanh(-log_a) * (a * a + 1.0))
        bt = mult * (ig * xc)
        hprev = hlru_s[h]
        for s in range(CHUNK):
            sl = slice(s * SUBLANES, (s + 1) * SUBLANES)
            hprev = a[sl] * hprev + bt[sl]
            hl_s[sl, :] = hprev
        hlru_s[h] = hprev
        y_lru = _rms(hl_s[...], lnw_ref[h]) * _silu(z_lru)

        q = qkv[:, :HEAD_DIM]
        k = qkv[:, HEAD_DIM:2 * HEAD_DIM]
        v = qkv[:, 2 * HEAD_DIM:]
        q = q * lax.rsqrt(jnp.sum(q * q, axis=-1, keepdims=True) + EPS) * (HEAD_DIM ** -0.5)
        k = k * lax.rsqrt(jnp.sum(k * k, axis=-1, keepdims=True) + EPS)
        sel = jnp.where(ecol < LANES, h, h + N_HEADS)
        e1 = jnp.where(erow == sel, 1.0, 0.0).astype(BF16)
        bgb = _dot(bg3_s[...], jnp.concatenate([e1, e1, e1], axis=0))
        betab = bgb[:, :LANES]
        gcb = bgb[:, LANES:]
        eg = jnp.exp(gcb)
        glast = gcb[rows - SUBLANES:rows]
        kdf = jnp.exp(glast[None] - gcb.reshape(CHUNK, SUBLANES, LANES)).reshape(rows, LANES)
        kb = k * betab
        q_s[...] = q
        k_s[...] = k
        kb_s[...] = kb
        kbg_s[...] = kb * eg
        qg_s[...] = q * eg
        kd_s[...] = k * kdf
        vb_s[...] = v * betab
        gcb_s[...] = gcb

        for b in range(SUBLANES):
            rsel = pl.ds(b, CHUNK, stride=SUBLANES)
            st = st_s[h, b]
            m1 = _dot(jnp.concatenate([kbg_s[rsel, :], qg_s[rsel, :]], axis=0).astype(BF16),
                      st.astype(BF16))
            kb_b = k_s[rsel, :].astype(BF16)
            m2 = _dot_nt(jnp.concatenate([kb_s[rsel, :], q_s[rsel, :]], axis=0).astype(BF16), kb_b)
            gcb_b = gcb_s[rsel, :]
            dmask = jnp.exp(jnp.where(tri, gcb_b[:, :CHUNK] - gcb_b.T[:CHUNK, :], -jnp.inf))
            amat = jnp.where(strict, m2[:CHUNK] * dmask, 0.0)
            attn = m2[CHUNK:] * dmask
            xs = vb_s[rsel, :] - m1[:CHUNK]
            pw = amat.astype(BF16)
            xs = xs - _dot(pw, xs.astype(BF16))
            for _ in range(5):
                pw = _dot(pw, pw).astype(BF16)
                xs = xs + _dot(pw, xs.astype(BF16))
            xsb = xs.astype(BF16)
            od_s[rsel, :] = m1[CHUNK:] + _dot(attn.astype(BF16), xsb)
            eglast = jnp.exp(gcb_s[rows - SUBLANES + b:rows - SUBLANES + b + 1, :])
            st_s[h, b] = st * eglast + _dot_tn(kd_s[rsel, :].astype(BF16), xsb)

        y_dn = _rms(od_s[...], dnw_ref[...]) * _silu(z_dn)
        yh = jnp.concatenate([y_lru, y_dn], axis=1).astype(BF16)
        o_ref[...] += _dot(yh, wo_ref[h])
        return carry

    lax.fori_loop(0, N_HEADS, head_body, 0)

    if final:
        o_ref[...] = _rms(o_ref[...], fnw_ref[...])


def _vmem_full():
    return pl.BlockSpec(memory_space=pltpu.VMEM)


def _layer(x_tb, params, final_norm_w, *, final):
    n_rows, d_model = x_tb.shape
    rows = CHUNK * SUBLANES
    n_steps = n_rows // rows
    row_spec = pl.BlockSpec((rows, d_model), lambda i: (i, 0))
    slab = pltpu.VMEM((rows, LANES), F32)
    scratch = [
        pltpu.VMEM((rows, d_model), BF16),
        pltpu.VMEM((rows, 3 * LANES), BF16),
        pltpu.VMEM((CARRY_ROWS + rows, N_CONV_COLS), F32),
        pltpu.VMEM((N_HEADS, CARRY_ROWS, N_CONV_COLS), F32),
        slab,
        pltpu.VMEM((N_HEADS, SUBLANES, HEAD_DIM), F32),
        slab, slab, slab, slab, slab, slab, slab, slab, slab,
        pltpu.VMEM((N_HEADS, SUBLANES, HEAD_DIM, HEAD_DIM), F32),
    ]
    return pl.pallas_call(
        functools.partial(_layer_kernel, final=final),
        grid=(n_steps,),
        in_specs=[row_spec] + [_vmem_full()] * 14,
        out_specs=row_spec,
        out_shape=jax.ShapeDtypeStruct((n_rows, d_model), F32),
        scratch_shapes=scratch,
        compiler_params=pltpu.CompilerParams(
            dimension_semantics=("arbitrary",), vmem_limit_bytes=VMEM_LIMIT_BYTES),
        name="hybrid_layer_final" if final else "hybrid_layer",
    )(x_tb, *params, final_norm_w)


def _prep_layer(norm_w, w_in, lru_conv_w, lru_conv_b, lru_wa, lru_ba, lru_wx, lru_bx, lru_lambda,
                lru_norm_w, dn_conv_w, dn_A_log, dn_dt_bias, dn_norm_w, w_out):
    d_model = w_in.shape[0]
    w = HEAD_DIM * N_HEADS
    per_head = lambda t: t.reshape(t.shape[0], N_HEADS, HEAD_DIM)
    lx, lz, q, k, v, dz = (per_head(w_in[:, i * w:(i + 1) * w]) for i in range(6))
    wh = jnp.concatenate([lx, q, k, v, lz, dz], axis=2).transpose(1, 0, 2).astype(BF16)
    wba = jnp.zeros((d_model, LANES), F32).at[:, :2 * N_HEADS].set(w_in[:, 6 * w:]).astype(BF16)
    cq, ck, cv = (per_head(dn_conv_w[:, i * w:(i + 1) * w]) for i in range(3))
    cw = jnp.concatenate([per_head(lru_conv_w), cq, ck, cv], axis=2).transpose(1, 0, 2)
    row = lambda t: t.reshape(N_HEADS, 1, HEAD_DIM)
    wg = jnp.concatenate([lru_wa, lru_wx], axis=2).astype(BF16)
    bg = jnp.concatenate([row(lru_ba), row(lru_bx)], axis=2)
    alog = jnp.zeros((1, LANES), F32).at[0, N_HEADS:2 * N_HEADS].set(dn_A_log)
    dt = jnp.zeros((1, LANES), F32).at[0, N_HEADS:2 * N_HEADS].set(dn_dt_bias)
    wo = jnp.concatenate([w_out[:w].reshape(N_HEADS, HEAD_DIM, d_model),
                          w_out[w:].reshape(N_HEADS, HEAD_DIM, d_model)], axis=1).astype(BF16)
    return (norm_w.reshape(1, d_model), wh, wba, cw, row(lru_conv_b), wg, bg, row(lru_lambda),
            row(lru_norm_w), dn_norm_w.reshape(1, HEAD_DIM), alog, dt, wo)


def kernel(x, norm_w, w_in, lru_conv_w, lru_conv_b, lru_wa, lru_ba, lru_wx, lru_bx, lru_lambda,
           lru_norm_w, dn_conv_w, dn_A_log, dn_dt_bias, dn_norm_w, w_out, final_norm_w):
    batch, seq, d_model = x.shape
    depth = norm_w.shape[0]
    assert batch == SUBLANES and seq % CHUNK == 0
    assert w_in.shape[2] == 6 * N_HEADS * HEAD_DIM + 2 * N_HEADS
    h = x.transpose(1, 0, 2).reshape(seq * batch, d_model)
    fnw = final_norm_w.reshape(1, d_model)
    for l in range(depth):
        params = _prep_layer(norm_w[l], w_in[l], lru_conv_w[l], lru_conv_b[l], lru_wa[l], lru_ba[l],
                             lru_wx[l], lru_bx[l], lru_lambda[l], lru_norm_w[l], dn_conv_w[l],
                             dn_A_log[l], dn_dt_bias[l], dn_norm_w[l], w_out[l])
        h = _layer(h, params, fnw, final=(l == depth - 1))
    return h.reshape(seq, batch, d_model).transpose(1, 0, 2)
```

```python
import functools

import jax
import jax.numpy as jnp
from jax import lax
from jax.experimental import pallas as pl
from jax.experimental.pallas import tpu as pltpu

F32 = jnp.float32
BF16 = jnp.bfloat16

N_HEADS = 8
HEAD_DIM = 128
CONV_WIDTH = 4
LRU_C = 8.0
CHUNK = 64
EPS = 1e-6
SUBLANES = 8
LANES = 128
N_CONV_COLS = 4 * HEAD_DIM
N_HEAD_COLS = 6 * HEAD_DIM
CARRY_ROWS = (CONV_WIDTH - 1) * SUBLANES
VMEM_LIMIT_BYTES = 56 * 1024 * 1024


_sigmoid = jax.nn.sigmoid
_silu = jax.nn.silu
_softplus = jax.nn.softplus


def _dot(a, b):
    return jnp.dot(a, b, preferred_element_type=F32)


def _dot_nt(a, b):
    return lax.dot_general(a, b, (((1,), (1,)), ((), ())), preferred_element_type=F32)


def _dot_tn(a, b):
    return lax.dot_general(a, b, (((0,), (0,)), ((), ())), preferred_element_type=F32)


def _rms(x, w):
    return x * lax.rsqrt(jnp.mean(x * x, axis=-1, keepdims=True) + EPS) * w


def _layer_kernel(x_ref, nw_ref, wh_ref, wba_ref, cw_ref, cb_ref, wg_ref, bg_ref, lam_ref,
                  lnw_ref, dnw_ref, alog_ref, dt_ref, wo_ref, fnw_ref,
                  o_ref,
                  xn_s, bg3_s, p_s, carry_s, hl_s, hlru_s, q_s, k_s, kb_s, kbg_s, qg_s, kd_s,
                  vb_s, gcb_s, od_s, st_s, *, final):
    rows = CHUNK * SUBLANES
    step = pl.program_id(0)

    @pl.when(step == 0)
    def _():
        carry_s[...] = jnp.zeros_like(carry_s)
        hlru_s[...] = jnp.zeros_like(hlru_s)
        st_s[...] = jnp.zeros_like(st_s)

    x = x_ref[...]
    xn_s[...] = _rms(x, nw_ref[...]).astype(BF16)
    o_ref[...] = x

    ba = _dot(xn_s[...], wba_ref[...])
    beta = _sigmoid(ba)
    g = -jnp.exp(alog_ref[...]) * _softplus(ba + dt_ref[...])
    acc = jnp.zeros((SUBLANES, LANES), F32)
    pieces = []
    for s in range(CHUNK):
        acc = acc + g[s * SUBLANES:(s + 1) * SUBLANES]
        pieces.append(acc)
    gc = jnp.concatenate(pieces, axis=0)
    lane = lax.broadcasted_iota(jnp.int32, (rows, LANES), 1)
    bgv = jnp.where(lane < N_HEADS, beta, gc)
    hi = bgv.astype(BF16)
    r1 = bgv - hi.astype(F32)
    mid = r1.astype(BF16)
    lo = (r1 - mid.astype(F32)).astype(BF16)
    bg3_s[...] = jnp.concatenate([hi, mid, lo], axis=1)

    ii = lax.broadcasted_iota(jnp.int32, (CHUNK, CHUNK), 0)
    jj = lax.broadcasted_iota(jnp.int32, (CHUNK, CHUNK), 1)
    tri = ii >= jj
    strict = ii > jj
    erow = lax.broadcasted_iota(jnp.int32, (LANES, 2 * LANES), 0)
    ecol = lax.broadcasted_iota(jnp.int32, (LANES, 2 * LANES), 1)

    def head_body(h, carry):
        pf = _dot(xn_s[...], wh_ref[h])
        pc = pf[:, :N_CONV_COLS]
        p_s[0:CARRY_ROWS, :] = carry_s[h]
        p_s[CARRY_ROWS:CARRY_ROWS + rows, :] = pc
        carry_s[h] = pc[rows - CARRY_ROWS:rows, :]
        cwh = cw_ref[h]
        y = cwh[CONV_WIDTH - 1:CONV_WIDTH] * pc
        for j in range(CONV_WIDTH - 1):
            y = y + cwh[j:j + 1] * p_s[j * SUBLANES:j * SUBLANES + rows, :]
        xc = y[:, :HEAD_DIM] + cb_ref[h]
        qkv = _silu(y[:, HEAD_DIM:])
        z_lru = pf[:, N_CONV_COLS:N_CONV_COLS + HEAD_DIM]
        z_dn = pf[:, N_CONV_COLS + HEAD_DIM:]

        gates = _dot(xc.astype(BF16), wg_ref[h]) + bg_ref[h]
        r = _sigmoid(gates[:, :HEAD_DIM])
        ig = _sigmoid(gates[:, HEAD_DIM:])
        log_a = r * (-LRU_C * _softplus(-lam_ref[h]))
        a = jnp.exp(log_a)
        mult = jnp.sqrt(jnp.tanh(-log_a) * (a * a + 1.0))
        bt = mult * (ig * xc)
        hprev = hlru_s[h]
        for s in range(CHUNK):
            sl = slice(s * SUBLANES, (s + 1) * SUBLANES)
            hprev = a[sl] * hprev + bt[sl]
            hl_s[sl, :] = hprev
        hlru_s[h] = hprev
        y_lru = _rms(hl_s[...], lnw_ref[h]) * _silu(z_lru)

        q = qkv[:, :HEAD_DIM]
        k = qkv[:, HEAD_DIM:2 * HEAD_DIM]
        v = qkv[:, 2 * HEAD_DIM:]
        q = q * lax.rsqrt(jnp.sum(q * q, axis=-1, keepdims=True) + EPS) * (HEAD_DIM ** -0.5)
        k = k * lax.rsqrt(jnp.sum(k * k, axis=-1, keepdims=True) + EPS)
        sel = jnp.where(ecol < LANES, h, h + N_HEADS)
        e1 = jnp.where(erow == sel, 1.0, 0.0).astype(BF16)
        bgb = _dot(bg3_s[...], jnp.concatenate([e1, e1, e1], axis=0))
        betab = bgb[:, :LANES]
        gcb = bgb[:, LANES:]
        eg = jnp.exp(gcb)
        glast = gcb[rows - SUBLANES:rows]
        kdf = jnp.exp(glast[None] - gcb.reshape(CHUNK, SUBLANES, LANES)).reshape(rows, LANES)
        kb = k * betab
        q_s[...] = q
        k_s[...] = k
        kb_s[...] = kb
        kbg_s[...] = kb * eg
        qg_s[...] = q * eg
        kd_s[...] = k * kdf
        vb_s[...] = v * betab
        gcb_s[...] = gcb

        bs = range(SUBLANES)
        rsel = [pl.ds(b, CHUNK, stride=SUBLANES) for b in bs]
        st = [st_s[h, b] for b in bs]
        m1 = [_dot(jnp.concatenate([kbg_s[rsel[b], :], qg_s[rsel[b], :]], axis=0).astype(BF16),
                   st[b].astype(BF16)) for b in bs]
        m2 = [_dot_nt(jnp.concatenate([kb_s[rsel[b], :], q_s[rsel[b], :]], axis=0).astype(BF16),
                      k_s[rsel[b], :].astype(BF16)) for b in bs]
        dmask = []
        for b in bs:
            gcb_b = gcb_s[rsel[b], :]
            dmask.append(jnp.exp(jnp.where(tri, gcb_b[:, :CHUNK] - gcb_b.T[:CHUNK, :], -jnp.inf)))
        pw = [jnp.where(strict, m2[b][:CHUNK] * dmask[b], 0.0).astype(BF16) for b in bs]
        attn = [(m2[b][CHUNK:] * dmask[b]).astype(BF16) for b in bs]
        xs = [vb_s[rsel[b], :] - m1[b][:CHUNK] for b in bs]
        xs = [xs[b] - _dot(pw[b], xs[b].astype(BF16)) for b in bs]
        for _ in range(5):
            pw = [_dot(pw[b], pw[b]).astype(BF16) for b in bs]
            xs = [xs[b] + _dot(pw[b], xs[b].astype(BF16)) for b in bs]
        xsb = [xs[b].astype(BF16) for b in bs]
        for b in bs:
            od_s[rsel[b], :] = m1[b][CHUNK:] + _dot(attn[b], xsb[b])
        for b in bs:
            eglast = jnp.exp(gcb_s[rows - SUBLANES + b:rows - SUBLANES + b + 1, :])
            st_s[h, b] = st[b] * eglast + _dot_tn(kd_s[rsel[b], :].astype(BF16), xsb[b])

        y_dn = _rms(od_s[...], dnw_ref[...]) * _silu(z_dn)
        yh = jnp.concatenate([y_lru, y_dn], axis=1).astype(BF16)
        o_ref[...] += _dot(yh, wo_ref[h])
        return carry

    lax.fori_loop(0, N_HEADS, head_body, 0)

    if final:
        o_ref[...] = _rms(o_ref[...], fnw_ref[...])


def _vmem_full():
    return pl.BlockSpec(memory_space=pltpu.VMEM)


def _layer(x_tb, params, final_norm_w, *, final):
    n_rows, d_model = x_tb.shape
    rows = CHUNK * SUBLANES
    n_steps = n_rows // rows
    row_spec = pl.BlockSpec((rows, d_model), lambda i: (i, 0))
    slab = pltpu.VMEM((rows, LANES), F32)
    scratch = [
        pltpu.VMEM((rows, d_model), BF16),
        pltpu.VMEM((rows, 3 * LANES), BF16),
        pltpu.VMEM((CARRY_ROWS + rows, N_CONV_COLS), F32),
        pltpu.VMEM((N_HEADS, CARRY_ROWS, N_CONV_COLS), F32),
        slab,
        pltpu.VMEM((N_HEADS, SUBLANES, HEAD_DIM), F32),
        slab, slab, slab, slab, slab, slab, slab, slab, slab,
        pltpu.VMEM((N_HEADS, SUBLANES, HEAD_DIM, HEAD_DIM), F32),
    ]
    return pl.pallas_call(
        functools.partial(_layer_kernel, final=final),
        grid=(n_steps,),
        in_specs=[row_spec] + [_vmem_full()] * 14,
        out_specs=row_spec,
        out_shape=jax.ShapeDtypeStruct((n_rows, d_model), F32),
        scratch_shapes=scratch,
        compiler_params=pltpu.CompilerParams(
            dimension_semantics=("arbitrary",), vmem_limit_bytes=VMEM_LIMIT_BYTES),
        name="hybrid_layer_final" if final else "hybrid_layer",
    )(x_tb, *params, final_norm_w)


def _prep_layer(norm_w, w_in, lru_conv_w, lru_conv_b, lru_wa, lru_ba, lru_wx, lru_bx, lru_lambda,
                lru_norm_w, dn_conv_w, dn_A_log, dn_dt_bias, dn_norm_w, w_out):
    d_model = w_in.shape[0]
    w = HEAD_DIM * N_HEADS
    per_head = lambda t: t.reshape(t.shape[0], N_HEADS, HEAD_DIM)
    lx, lz, q, k, v, dz = (per_head(w_in[:, i * w:(i + 1) * w]) for i in range(6))
    wh = jnp.concatenate([lx, q, k, v, lz, dz], axis=2).transpose(1, 0, 2).astype(BF16)
    wba = jnp.zeros((d_model, LANES), F32).at[:, :2 * N_HEADS].set(w_in[:, 6 * w:]).astype(BF16)
    cq, ck, cv = (per_head(dn_conv_w[:, i * w:(i + 1) * w]) for i in range(3))
    cw = jnp.concatenate([per_head(lru_conv_w), cq, ck, cv], axis=2).transpose(1, 0, 2)
    row = lambda t: t.reshape(N_HEADS, 1, HEAD_DIM)
    wg = jnp.concatenate([lru_wa, lru_wx], axis=2).astype(BF16)
    bg = jnp.concatenate([row(lru_ba), row(lru_bx)], axis=2)
    alog = jnp.zeros((1, LANES), F32).at[0, N_HEADS:2 * N_HEADS].set(dn_A_log)
    dt = jnp.zeros((1, LANES), F32).at[0, N_HEADS:2 * N_HEADS].set(dn_dt_bias)
    wo = jnp.concatenate([w_out[:w].reshape(N_HEADS, HEAD_DIM, d_model),
                          w_out[w:].reshape(N_HEADS, HEAD_DIM, d_model)], axis=1).astype(BF16)
    return (norm_w.reshape(1, d_model), wh, wba, cw, row(lru_conv_b), wg, bg, row(lru_lambda),
            row(lru_norm_w), dn_norm_w.reshape(1, HEAD_DIM), alog, dt, wo)


def kernel(x, norm_w, w_in, lru_conv_w, lru_conv_b, lru_wa, lru_ba, lru_wx, lru_bx, lru_lambda,
           lru_norm_w, dn_conv_w, dn_A_log, dn_dt_bias, dn_norm_w, w_out, final_norm_w):
    batch, seq, d_model = x.shape
    depth = norm_w.shape[0]
    assert batch == SUBLANES and seq % CHUNK == 0
    assert w_in.shape[2] == 6 * N_HEADS * HEAD_DIM + 2 * N_HEADS
    h = x.transpose(1, 0, 2).reshape(seq * batch, d_model)
    fnw = final_norm_w.reshape(1, d_model)
    for l in range(depth):
        params = _prep_layer(norm_w[l], w_in[l], lru_conv_w[l], lru_conv_b[l], lru_wa[l], lru_ba[l],
                             lru_wx[l], lru_bx[l], lru_lambda[l], lru_norm_w[l], dn_conv_w[l],
                             dn_A_log[l], dn_dt_bias[l], dn_norm_w[l], w_out[l])
        h = _layer(h, params, fnw, final=(l == depth - 1))
    return h.reshape(seq, batch, d_model).transpose(1, 0, 2)
```

```python
import functools

import jax
import jax.numpy as jnp
from jax import lax
from jax.experimental import pallas as pl
from jax.experimental.pallas import tpu as pltpu

F32 = jnp.float32
BF16 = jnp.bfloat16

N_HEADS = 8
HEAD_DIM = 128
CONV_WIDTH = 4
LRU_C = 8.0
CHUNK = 64
EPS = 1e-6
SUBLANES = 8
LANES = 128
MXU_TILE = 256
N_CONV_COLS = 4 * HEAD_DIM
N_HEAD_COLS = 6 * HEAD_DIM
CARRY_ROWS = (CONV_WIDTH - 1) * SUBLANES
VMEM_LIMIT_BYTES = 56 * 1024 * 1024
HANDOVER = ("q", "k", "kb", "kbg", "qg", "kd", "vb", "gcb", "ylru", "zdn")


_sigmoid = jax.nn.sigmoid
_silu = jax.nn.silu
_softplus = jax.nn.softplus


def _dot(a, b):
    return jnp.dot(a, b, preferred_element_type=F32)


def _dot_nt(a, b):
    return lax.dot_general(a, b, (((1,), (1,)), ((), ())), preferred_element_type=F32)


def _dot_tn(a, b):
    return lax.dot_general(a, b, (((0,), (0,)), ((), ())), preferred_element_type=F32)


def _rms(x, w):
    return x * lax.rsqrt(jnp.mean(x * x, axis=-1, keepdims=True) + EPS) * w


def _interleave(*stage_generators):
    live = list(stage_generators)
    while live:
        for gen in list(live):
            try:
                next(gen)
            except StopIteration:
                live.remove(gen)


def _chain(*generator_fns):
    for fn in generator_fns:
        yield from fn()


def _layer_kernel(x_ref, xnext_ref, nw_ref, wh_ref, wba_ref, cw_ref, cb_ref, wg_ref, bg_ref,
                  lam_ref, lnw_ref, dnw_ref, alog_ref, dt_ref, wo_ref, fnw_ref,
                  o_ref,
                  xn_s, bgnext_s, bgcur_s, p_s, carry_s, hl_s, hlru_s, od_s, st_s, pf0_s, pf1_s,
                  *handover, final):
    rows = CHUNK * SUBLANES
    n_ho = len(HANDOVER)
    ho_slots = [dict(zip(HANDOVER, handover[:n_ho])), dict(zip(HANDOVER, handover[n_ho:]))]
    pf_slots = [pf0_s, pf1_s]
    step = pl.program_id(0)

    lane = lax.broadcasted_iota(jnp.int32, (rows, LANES), 1)
    ii = lax.broadcasted_iota(jnp.int32, (CHUNK, CHUNK), 0)
    jj = lax.broadcasted_iota(jnp.int32, (CHUNK, CHUNK), 1)
    tri = ii >= jj
    strict = ii > jj

    def lane_bcast(vals, col):
        picked = jnp.sum(jnp.where(lane == col, vals, 0.0), axis=-1, keepdims=True)
        return jnp.broadcast_to(picked, vals.shape)

    def prepare(src_ref):
        xn_s[...] = _rms(src_ref[...], nw_ref[...]).astype(BF16)
        yield
        ba = _dot(xn_s[...], wba_ref[...])
        beta = _sigmoid(ba)
        g = -jnp.exp(alog_ref[...]) * _softplus(ba + dt_ref[...])
        acc = jnp.zeros((SUBLANES, LANES), F32)
        pieces = []
        for s in range(CHUNK):
            acc = acc + g[s * SUBLANES:(s + 1) * SUBLANES]
            pieces.append(acc)
        gc = jnp.concatenate(pieces, axis=0)
        bgnext_s[...] = jnp.where(lane < N_HEADS, beta, gc)
        yield

    def proj(h, pf_ref):
        for c in range(N_HEAD_COLS // MXU_TILE):
            cols = slice(c * MXU_TILE, (c + 1) * MXU_TILE)
            pf_ref[:, cols] = _dot(xn_s[...], wh_ref[h, :, cols])
            yield

    def front(h, pf_ref, bgsrc_ref, ho):
        pc = pf_ref[:, :N_CONV_COLS]
        p_s[0:CARRY_ROWS, :] = carry_s[h]
        p_s[CARRY_ROWS:CARRY_ROWS + rows, :] = pc
        carry_s[h] = pc[rows - CARRY_ROWS:rows, :]
        cwh = cw_ref[h]
        y = cwh[CONV_WIDTH - 1:CONV_WIDTH] * pc
        for j in range(CONV_WIDTH - 1):
            y = y + cwh[j:j + 1] * p_s[j * SUBLANES:j * SUBLANES + rows, :]
        xc = y[:, :HEAD_DIM] + cb_ref[h]
        qkv = _silu(y[:, HEAD_DIM:])
        yield

        gates = _dot(xc.astype(BF16), wg_ref[h]) + bg_ref[h]
        r = _sigmoid(gates[:, :HEAD_DIM])
        ig = _sigmoid(gates[:, HEAD_DIM:])
        log_a = r * (-LRU_C * _softplus(-lam_ref[h]))
        a = jnp.exp(log_a)
        mult = jnp.sqrt(jnp.tanh(-log_a) * (a * a + 1.0))
        bt = mult * (ig * xc)
        yield
        hprev = hlru_s[h]
        for s in range(CHUNK):
            sl = slice(s * SUBLANES, (s + 1) * SUBLANES)
            hprev = a[sl] * hprev + bt[sl]
            hl_s[sl, :] = hprev
        hlru_s[h] = hprev
        z_lru = pf_ref[:, N_CONV_COLS:N_CONV_COLS + HEAD_DIM]
        ho["ylru"][...] = _rms(hl_s[...], lnw_ref[h]) * _silu(z_lru)
        yield

        q = qkv[:, :HEAD_DIM]
        k = qkv[:, HEAD_DIM:2 * HEAD_DIM]
        v = qkv[:, 2 * HEAD_DIM:]
        q = q * lax.rsqrt(jnp.sum(q * q, axis=-1, keepdims=True) + EPS) * (HEAD_DIM ** -0.5)
        k = k * lax.rsqrt(jnp.sum(k * k, axis=-1, keepdims=True) + EPS)
        bgv = bgsrc_ref[...]
        betab = lane_bcast(bgv, h)
        gcb = lane_bcast(bgv, h + N_HEADS)
        yield
        eg = jnp.exp(gcb)
        glast = gcb[rows - SUBLANES:rows]
        kdf = jnp.exp(glast[None] - gcb.reshape(CHUNK, SUBLANES, LANES)).reshape(rows, LANES)
        kb = k * betab
        ho["q"][...] = q
        ho["k"][...] = k
        ho["kb"][...] = kb
        ho["kbg"][...] = kb * eg
        ho["qg"][...] = q * eg
        ho["kd"][...] = k * kdf
        ho["vb"][...] = v * betab
        ho["gcb"][...] = gcb
        ho["zdn"][...] = pf_ref[:, N_CONV_COLS + HEAD_DIM:]
        yield

    def back(h, ho):
        bs = range(SUBLANES)
        rsel = [pl.ds(b, CHUNK, stride=SUBLANES) for b in bs]
        st = [st_s[h, b] for b in bs]
        m1 = [_dot(jnp.concatenate([ho["kbg"][rsel[b], :], ho["qg"][rsel[b], :]],
                                   axis=0).astype(BF16), st[b].astype(BF16)) for b in bs]
        m2 = [_dot_nt(jnp.concatenate([ho["kb"][rsel[b], :], ho["q"][rsel[b], :]],
                                      axis=0).astype(BF16), ho["k"][rsel[b], :].astype(BF16))
              for b in bs]
        yield
        dmask = []
        for b in bs:
            gcb_b = ho["gcb"][rsel[b], :]
            dmask.append(jnp.exp(jnp.where(tri, gcb_b[:, :CHUNK] - gcb_b.T[:CHUNK, :], -jnp.inf)))
        pw = [jnp.where(strict, m2[b][:CHUNK] * dmask[b], 0.0).astype(BF16) for b in bs]
        attn = [(m2[b][CHUNK:] * dmask[b]).astype(BF16) for b in bs]
        xs = [ho["vb"][rsel[b], :] - m1[b][:CHUNK] for b in bs]
        xs = [xs[b] - _dot(pw[b], xs[b].astype(BF16)) for b in bs]
        for _ in range(5):
            pw = [_dot(pw[b], pw[b]).astype(BF16) for b in bs]
            yield
            xs = [xs[b] + _dot(pw[b], xs[b].astype(BF16)) for b in bs]
        yield
        xsb = [xs[b].astype(BF16) for b in bs]
        for b in bs:
            od_s[rsel[b], :] = m1[b][CHUNK:] + _dot(attn[b], xsb[b])
        for b in bs:
            eglast = jnp.exp(ho["gcb"][rows - SUBLANES + b:rows - SUBLANES + b + 1, :])
            st_s[h, b] = st[b] * eglast + _dot_tn(ho["kd"][rsel[b], :].astype(BF16), xsb[b])
        yield
        y_dn = _rms(od_s[...], dnw_ref[...]) * _silu(ho["zdn"][...])
        yh =jnp.concatenate([ho["ylru"][...], y_dn], axis=1).astype(BF16)
        o_ref[...] += _dot(yh, wo_ref[h])
        yield

    @pl.when(step == 0)
    def _():
        carry_s[...] = jnp.zeros_like(carry_s)
        hlru_s[...] = jnp.zeros_like(hlru_s)
        st_s[...] = jnp.zeros_like(st_s)
        _interleave(_chain(lambda: prepare(x_ref),
                           lambda: proj(0, pf_slots[0]),
                           lambda: proj(1, pf_slots[1]),
                           lambda: front(0, pf_slots[0], bgnext_s, ho_slots[0])))

    bgcur_s[...] = bgnext_s[...]
    o_ref[...] = x_ref[...]

    def item_pair(j, carry):
        h = 2 * j
        _interleave(back(h, ho_slots[0]), front(h + 1, pf_slots[1], bgcur_s, ho_slots[1]),
                    proj(h + 2, pf_slots[0]))
        _interleave(back(h + 1, ho_slots[1]), front(h + 2, pf_slots[0], bgcur_s, ho_slots[0]),
                    proj(h + 3, pf_slots[1]))
        return carry

    lax.fori_loop(0, (N_HEADS - 2) // 2, item_pair, 0)
    _interleave(back(N_HEADS - 2, ho_slots[0]),
                front(N_HEADS - 1, pf_slots[1], bgcur_s, ho_slots[1]),
                _chain(lambda: prepare(xnext_ref), lambda: proj(0, pf_slots[0])))
    _interleave(back(N_HEADS - 1, ho_slots[1]),
                front(0, pf_slots[0], bgnext_s, ho_slots[0]),
                proj(1, pf_slots[1]))

    if final:
        o_ref[...] = _rms(o_ref[...], fnw_ref[...])


def _vmem_full():
    return pl.BlockSpec(memory_space=pltpu.VMEM)


def _layer(x_tb, params, final_norm_w, *, final):
    n_rows, d_model = x_tb.shape
    rows = CHUNK * SUBLANES
    n_steps = n_rows // rows
    row_spec = pl.BlockSpec((rows, d_model), lambda i: (i, 0))
    next_row_spec = pl.BlockSpec((rows, d_model), lambda i: (jnp.minimum(i + 1, n_steps - 1), 0))
    slab = pltpu.VMEM((rows, LANES), F32)
    pf_buf = pltpu.VMEM((rows, N_HEAD_COLS), F32)
    scratch = [
        pltpu.VMEM((rows, d_model), BF16),
        slab, slab,
        pltpu.VMEM((CARRY_ROWS + rows, N_CONV_COLS), F32),
        pltpu.VMEM((N_HEADS, CARRY_ROWS, N_CONV_COLS), F32),
        slab,
        pltpu.VMEM((N_HEADS, SUBLANES, HEAD_DIM), F32),
        slab,
        pltpu.VMEM((N_HEADS, SUBLANES, HEAD_DIM, HEAD_DIM), F32),
        pf_buf, pf_buf,
    ] + [slab] * (2 * len(HANDOVER))
    return pl.pallas_call(
        functools.partial(_layer_kernel, final=final),
        grid=(n_steps,),
        in_specs=[row_spec, next_row_spec] + [_vmem_full()] * 14,
        out_specs=row_spec,
        out_shape=jax.ShapeDtypeStruct((n_rows, d_model), F32),
        scratch_shapes=scratch,
        compiler_params=pltpu.CompilerParams(
            dimension_semantics=("arbitrary",), vmem_limit_bytes=VMEM_LIMIT_BYTES),
        name="hybrid_layer_final" if final else "hybrid_layer",
    )(x_tb, x_tb, *params, final_norm_w)


def _prep_layer(norm_w, w_in, lru_conv_w, lru_conv_b, lru_wa, lru_ba, lru_wx, lru_bx, lru_lambda,
                lru_norm_w, dn_conv_w, dn_A_log, dn_dt_bias, dn_norm_w, w_out):
    d_model = w_in.shape[0]
    w = HEAD_DIM * N_HEADS
    per_head = lambda t: t.reshape(t.shape[0], N_HEADS, HEAD_DIM)
    lx, lz, q, k, v, dz = (per_head(w_in[:, i * w:(i + 1) * w]) for i in range(6))
    wh = jnp.concatenate([lx, q, k, v, lz, dz], axis=2).transpose(1, 0, 2).astype(BF16)
    wba = jnp.zeros((d_model, LANES), F32).at[:, :2 * N_HEADS].set(w_in[:, 6 * w:]).astype(BF16)
    cq, ck, cv = (per_head(dn_conv_w[:, i * w:(i + 1) * w]) for i in range(3))
    cw = jnp.concatenate([per_head(lru_conv_w), cq, ck, cv], axis=2).transpose(1, 0, 2)
    row = lambda t: t.reshape(N_HEADS, 1, HEAD_DIM)
    wg = jnp.concatenate([lru_wa, lru_wx], axis=2).astype(BF16)
    bg = jnp.concatenate([row(lru_ba), row(lru_bx)], axis=2)
    alog = jnp.zeros((1, LANES), F32).at[0, N_HEADS:2 * N_HEADS].set(dn_A_log)
    dt = jnp.zeros((1, LANES), F32).at[0, N_HEADS:2 * N_HEADS].set(dn_dt_bias)
    wo = jnp.concatenate([w_out[:w].reshape(N_HEADS, HEAD_DIM, d_model),
                          w_out[w:].reshape(N_HEADS, HEAD_DIM, d_model)], axis=1).astype(BF16)
    return (norm_w.reshape(1, d_model), wh, wba, cw, row(lru_conv_b), wg, bg, row(lru_lambda),
            row(lru_norm_w), dn_norm_w.reshape(1, HEAD_DIM), alog, dt, wo)


def kernel(x, norm_w, w_in, lru_conv_w, lru_conv_b, lru_wa, lru_ba, lru_wx, lru_bx, lru_lambda,
           lru_norm_w, dn_conv_w, dn_A_log, dn_dt_bias, dn_norm_w, w_out, final_norm_w):
    batch, seq, d_model = x.shape
    depth = norm_w.shape[0]
    assert batch == SUBLANES and seq % CHUNK == 0
    assert w_in.shape[2] == 6 * N_HEADS * HEAD_DIM + 2 * N_HEADS
    h = x.transpose(1, 0, 2).reshape(seq * batch, d_model)
    fnw = final_norm_w.reshape(1, d_model)
    for l in range(depth):
        params = _prep_layer(norm_w[l], w_in[l], lru_conv_w[l], lru_conv_b[l], lru_wa[l], lru_ba[l],
                             lru_wx[l], lru_bx[l], lru_lambda[l], lru_norm_w[l], dn_conv_w[l],
                             dn_A_log[l], dn_dt_bias[l], dn_norm_w[l], w_out[l])
        h = _layer(h, params, fnw, final=(l == depth - 1))
    return h.reshape(seq, batch, d_model).transpose(1, 0, 2)
```

```python
import functools

import jax
import jax.numpy as jnp
from jax import lax
from jax.experimental import pallas as pl
from jax.experimental.pallas import tpu as pltpu

F32 = jnp.float32
BF16 = jnp.bfloat16

N_HEADS = 8
HEAD_DIM = 128
CONV_WIDTH = 4
LRU_C = 8.0
CHUNK = 64
EPS = 1e-6
SUBLANES = 8
LANES = 128
MXU_TILE = 256
N_CONV_COLS = 4 * HEAD_DIM
N_HEAD_COLS = 6 * HEAD_DIM
CARRY_ROWS = (CONV_WIDTH - 1) * SUBLANES
VMEM_LIMIT_BYTES = 60 * 1024 * 1024
HANDOVER = ("q", "k", "kb", "kbg", "qg", "kd", "vb", "gcb", "ylru", "zdn")
N_HANDOVER_SLOTS = 4


_sigmoid = jax.nn.sigmoid
_silu = jax.nn.silu
_softplus = jax.nn.softplus


def _dot(a, b):
    return jnp.dot(a, b, preferred_element_type=F32)


def _dot_nt(a, b):
    return lax.dot_general(a, b, (((1,), (1,)), ((), ())), preferred_element_type=F32)


def _dot_tn(a, b):
    return lax.dot_general(a, b, (((0,), (0,)), ((), ())), preferred_element_type=F32)


def _rms(x, w):
    return x * lax.rsqrt(jnp.mean(x * x, axis=-1, keepdims=True) + EPS) * w


def _interleave(*stage_generators):
    live = list(stage_generators)
    while live:
        for gen in list(live):
            try:
                next(gen)
            except StopIteration:
                live.remove(gen)


def _chain(*generator_fns):
    for fn in generator_fns:
        yield from fn()


def _layer_kernel(x_ref, xnext_ref, nw_ref, wh_ref, wba_ref, cw_ref, cb_ref, wg_ref, bg_ref,
                  lam_ref, lnw_ref, dnw_ref, alog_ref, dt_ref, wo_ref, fnw_ref,
                  o_ref,
                  xn_s, bgnext_s, bgcur_s, p_s, carry_s, hl_s, hlru_s, od0_s, od1_s, st_s,
                  pf0_s, pf1_s, *handover, final):
    rows = CHUNK * SUBLANES
    n_bufs = len(HANDOVER)
    ho_slots = [dict(zip(HANDOVER, handover[s * n_bufs:(s + 1) * n_bufs]))
                for s in range(N_HANDOVER_SLOTS)]
    pf_slots = [pf0_s, pf1_s]
    od_s = [od0_s, od1_s]
    step = pl.program_id(0)

    lane = lax.broadcasted_iota(jnp.int32, (rows, LANES), 1)
    ii = lax.broadcasted_iota(jnp.int32, (CHUNK, CHUNK), 0)
    jj = lax.broadcasted_iota(jnp.int32, (CHUNK, CHUNK), 1)
    tri = ii >= jj
    strict = ii > jj

    def lane_bcast(vals, col):
        picked = jnp.sum(jnp.where(lane == col, vals, 0.0), axis=-1, keepdims=True)
        return jnp.broadcast_to(picked, vals.shape)

    def prepare(src_ref):
        xn_s[...] = _rms(src_ref[...], nw_ref[...]).astype(BF16)
        yield
        ba = _dot(xn_s[...], wba_ref[...])
        beta = _sigmoid(ba)
        g = -jnp.exp(alog_ref[...]) * _softplus(ba + dt_ref[...])
        acc = jnp.zeros((SUBLANES, LANES), F32)
        pieces = []
        for s in range(CHUNK):
            acc = acc + g[s * SUBLANES:(s + 1) * SUBLANES]
            pieces.append(acc)
        gc = jnp.concatenate(pieces, axis=0)
        bgnext_s[...] = jnp.where(lane < N_HEADS, beta, gc)
        yield

    def proj(h, pf_ref):
        for c in range(N_HEAD_COLS // MXU_TILE):
            cols = slice(c * MXU_TILE, (c + 1) * MXU_TILE)
            pf_ref[:, cols] = _dot(xn_s[...], wh_ref[h, :, cols])
            yield

    def front(h, pf_ref, bgsrc_ref, ho):
        pc = pf_ref[:, :N_CONV_COLS]
        p_s[0:CARRY_ROWS, :] = carry_s[h]
        p_s[CARRY_ROWS:CARRY_ROWS + rows, :] = pc
        carry_s[h] = pc[rows - CARRY_ROWS:rows, :]
        cwh = cw_ref[h]
        y = cwh[CONV_WIDTH - 1:CONV_WIDTH] * pc
        for j in range(CONV_WIDTH - 1):
            y = y + cwh[j:j + 1] * p_s[j * SUBLANES:j * SUBLANES + rows, :]
        xc = y[:, :HEAD_DIM] + cb_ref[h]
        qkv = _silu(y[:, HEAD_DIM:])
        yield

        gates = _dot(xc.astype(BF16), wg_ref[h]) + bg_ref[h]
        r = _sigmoid(gates[:, :HEAD_DIM])
        ig = _sigmoid(gates[:, HEAD_DIM:])
        log_a = r * (-LRU_C * _softplus(-lam_ref[h]))
        a = jnp.exp(log_a)
        mult = jnp.sqrt(jnp.tanh(-log_a) * (a * a + 1.0))
        bt = mult * (ig * xc)
        yield
        hprev = hlru_s[h]
        for s in range(CHUNK):
            sl = slice(s * SUBLANES, (s + 1) * SUBLANES)
            hprev = a[sl] * hprev + bt[sl]
            hl_s[sl, :] = hprev
        hlru_s[h] = hprev
        z_lru = pf_ref[:, N_CONV_COLS:N_CONV_COLS + HEAD_DIM]
        ho["ylru"][...] = _rms(hl_s[...], lnw_ref[h]) * _silu(z_lru)
        yield

        q = qkv[:, :HEAD_DIM]
        k = qkv[:, HEAD_DIM:2 * HEAD_DIM]
        v = qkv[:, 2 * HEAD_DIM:]
        q = q * lax.rsqrt(jnp.sum(q * q, axis=-1, keepdims=True) + EPS) * (HEAD_DIM ** -0.5)
        k = k * lax.rsqrt(jnp.sum(k * k, axis=-1, keepdims=True) + EPS)
        bgv = bgsrc_ref[...]
        betab = lane_bcast(bgv, h)
        gcb = lane_bcast(bgv, h + N_HEADS)
        yield
        eg = jnp.exp(gcb)
        glast = gcb[rows - SUBLANES:rows]
        kdf = jnp.exp(glast[None] - gcb.reshape(CHUNK, SUBLANES, LANES)).reshape(rows, LANES)
        kb = k * betab
        ho["q"][...] = q
        ho["k"][...] = k
        ho["kb"][...] = kb
        ho["kbg"][...] = kb * eg
        ho["qg"][...] = q * eg
        ho["kd"][...] = k * kdf
        ho["vb"][...] = v * betab
        ho["gcb"][...] = gcb
        ho["zdn"][...] = pf_ref[:, N_CONV_COLS + HEAD_DIM:]
        yield

    def back(h0, hos):
        n_hd = len(hos)
        cs = range(n_hd * SUBLANES)
        ho = [hos[c // SUBLANES] for c in cs]
        hd = [h0 + c // SUBLANES for c in cs]
        bt = [c % SUBLANES for c in cs]
        rsel = [pl.ds(bt[c], CHUNK, stride=SUBLANES) for c in cs]
        st = [st_s[hd[c], bt[c]] for c in cs]
        m1 = [_dot(jnp.concatenate([ho[c]["kbg"][rsel[c], :], ho[c]["qg"][rsel[c], :]],
                                   axis=0).astype(BF16), st[c].astype(BF16)) for c in cs]
        m2 = [_dot_nt(jnp.concatenate([ho[c]["kb"][rsel[c], :], ho[c]["q"][rsel[c], :]],
                                      axis=0).astype(BF16), ho[c]["k"][rsel[c], :].astype(BF16))
              for c in cs]
        yield
        dmask = []
        for c in cs:
            gcb_c = ho[c]["gcb"][rsel[c], :]
            dmask.append(jnp.exp(jnp.where(tri, gcb_c[:, :CHUNK] - gcb_c.T[:CHUNK, :], -jnp.inf)))
        pw = [jnp.where(strict, m2[c][:CHUNK] * dmask[c], 0.0).astype(BF16) for c in cs]
        attn = [(m2[c][CHUNK:] * dmask[c]).astype(BF16) for c in cs]
        xs = [ho[c]["vb"][rsel[c], :] - m1[c][:CHUNK] for c in cs]
        xs = [xs[c] - _dot(pw[c], xs[c].astype(BF16)) for c in cs]
        for _ in range(5):
            pw = [_dot(pw[c], pw[c]).astype(BF16) for c in cs]
            yield
            xs = [xs[c] + _dot(pw[c], xs[c].astype(BF16)) for c in cs]
        yield
        xsb = [xs[c].astype(BF16) for c in cs]
        for c in cs:
            od_s[c // SUBLANES][rsel[c], :] = m1[c][CHUNK:] + _dot(attn[c], xsb[c])
        for c in cs:
            last = rows - SUBLANES + bt[c]
            eglast = jnp.exp(ho[c]["gcb"][last:last + 1, :])
            st_s[hd[c], bt[c]] = (st[c] * eglast
                                  + _dot_tn(ho[c]["kd"][rsel[c], :].astype(BF16), xsb[c]))
        yield
        ys = []
        for i in range(n_hd):
            ys.append(hos[i]["ylru"][...])
            ys.append(_rms(od_s[i][...], dnw_ref[...]) * _silu(hos[i]["zdn"][...]))
        yh = jnp.concatenate(ys, axis=1).astype(BF16)
        wo = jnp.concatenate([wo_ref[h0 + i] for i in range(n_hd)], axis=0)
        o_ref[...] += _dot(yh, wo)
        yield

    @pl.when(step == 0)
    def _():
        carry_s[...] = jnp.zeros_like(carry_s)
        hlru_s[...] = jnp.zeros_like(hlru_s)
        st_s[...] = jnp.zeros_like(st_s)
        _interleave(_chain(lambda: prepare(x_ref),
                           lambda: proj(0, pf_slots[0]),
                           lambda: proj(1, pf_slots[1]),
                           lambda: front(0, pf_slots[0], bgnext_s, ho_slots[0])))

    bgcur_s[...] = bgnext_s[...]
    o_ref[...] = x_ref[...]

    n_ho = len(ho_slots)
    for i in range(N_HEADS):
        stages = []
        if i % 2 == 1:
            stages.append(back(i - 1, [ho_slots[(i - 1) % n_ho], ho_slots[i % n_ho]]))
        if i + 1 < N_HEADS:
            stages.append(front(i + 1, pf_slots[(i + 1) % 2], bgcur_s, ho_slots[(i + 1) % n_ho]))
        else:
            stages.append(front(0, pf_slots[0], bgnext_s, ho_slots[0]))
        if i + 2 < N_HEADS:
            stages.append(proj(i + 2, pf_slots[i % 2]))
        elif i + 2 == N_HEADS:
            stages.append(_chain(lambda: prepare(xnext_ref), lambda: proj(0, pf_slots[0])))
        else:
            stages.append(proj(1, pf_slots[1]))
        _interleave(*stages)

    if final:
        o_ref[...] = _rms(o_ref[...], fnw_ref[...])


def _vmem_full():
    return pl.BlockSpec(memory_space=pltpu.VMEM)


def _layer(x_tb, params, final_norm_w, *, final):
    n_rows, d_model = x_tb.shape
    rows = CHUNK * SUBLANES
    n_steps = n_rows // rows
    row_spec = pl.BlockSpec((rows, d_model), lambda i: (i, 0))
    next_row_spec = pl.BlockSpec((rows, d_model), lambda i: (jnp.minimum(i + 1, n_steps - 1), 0))
    slab = pltpu.VMEM((rows, LANES), F32)
    pf_buf = pltpu.VMEM((rows, N_HEAD_COLS), F32)
    scratch = [
        pltpu.VMEM((rows, d_model), BF16),
        slab, slab,
        pltpu.VMEM((CARRY_ROWS + rows, N_CONV_COLS), F32),
        pltpu.VMEM((N_HEADS, CARRY_ROWS, N_CONV_COLS), F32),
        slab,
        pltpu.VMEM((N_HEADS, SUBLANES, HEAD_DIM), F32),
        slab, slab,
        pltpu.VMEM((N_HEADS, SUBLANES, HEAD_DIM, HEAD_DIM), F32),
        pf_buf, pf_buf,
    ] + [slab] * (N_HANDOVER_SLOTS * len(HANDOVER))
    return pl.pallas_call(
        functools.partial(_layer_kernel, final=final),
        grid=(n_steps,),
        in_specs=[row_spec, next_row_spec] + [_vmem_full()] * 14,
        out_specs=row_spec,
        out_shape=jax.ShapeDtypeStruct((n_rows, d_model), F32),
        scratch_shapes=scratch,
        compiler_params=pltpu.CompilerParams(
            dimension_semantics=("arbitrary",), vmem_limit_bytes=VMEM_LIMIT_BYTES),
        name="hybrid_layer_final" if final else "hybrid_layer",
    )(x_tb, x_tb, *params, final_norm_w)


def _prep_layer(norm_w, w_in, lru_conv_w, lru_conv_b, lru_wa, lru_ba, lru_wx, lru_bx, lru_lambda,
                lru_norm_w, dn_conv_w, dn_A_log, dn_dt_bias, dn_norm_w, w_out):
    d_model = w_in.shape[0]
    w = HEAD_DIM * N_HEADS
    per_head = lambda t: t.reshape(t.shape[0], N_HEADS, HEAD_DIM)
    lx, lz, q, k, v, dz = (per_head(w_in[:, i * w:(i + 1) * w]) for i in range(6))
    wh = jnp.concatenate([lx, q, k, v, lz, dz], axis=2).transpose(1, 0, 2).astype(BF16)
    wba = jnp.zeros((d_model, LANES), F32).at[:, :2 * N_HEADS].set(w_in[:, 6 * w:]).astype(BF16)
    cq, ck, cv = (per_head(dn_conv_w[:, i * w:(i + 1) * w]) for i in range(3))
    cw = jnp.concatenate([per_head(lru_conv_w), cq, ck, cv], axis=2).transpose(1, 0, 2)
    row = lambda t: t.reshape(N_HEADS, 1, HEAD_DIM)
    wg = jnp.concatenate([lru_wa, lru_wx], axis=2).astype(BF16)
    bg = jnp.concatenate([row(lru_ba), row(lru_bx)], axis=2)
    alog = jnp.zeros((1, LANES), F32).at[0, N_HEADS:2 * N_HEADS].set(dn_A_log)
    dt = jnp.zeros((1, LANES), F32).at[0, N_HEADS:2 * N_HEADS].set(dn_dt_bias)
    wo = jnp.concatenate([w_out[:w].reshape(N_HEADS, HEAD_DIM, d_model),
                          w_out[w:].reshape(N_HEADS, HEAD_DIM, d_model)], axis=1).astype(BF16)
    return (norm_w.reshape(1, d_model), wh, wba, cw, row(lru_conv_b), wg, bg, row(lru_lambda),
            row(lru_norm_w), dn_norm_w.reshape(1, HEAD_DIM), alog, dt, wo)


def kernel(x, norm_w, w_in, lru_conv_w, lru_conv_b, lru_wa, lru_ba, lru_wx, lru_bx, lru_lambda,
           lru_norm_w, dn_conv_w, dn_A_log, dn_dt_bias, dn_norm_w, w_out, final_norm_w):
    batch, seq, d_model = x.shape
    depth = norm_w.shape[0]
    assert batch == SUBLANES and seq % CHUNK == 0
    assert w_in.shape[2] == 6 * N_HEADS * HEAD_DIM + 2 * N_HEADS
    h = x.transpose(1, 0, 2).reshape(seq * batch, d_model)
    fnw = final_norm_w.reshape(1, d_model)
    for l in range(depth):
        params = _prep_layer(norm_w[l], w_in[l], lru_conv_w[l], lru_conv_b[l], lru_wa[l], lru_ba[l],
                             lru_wx[l], lru_bx[l], lru_lambda[l], lru_norm_w[l], dn_conv_w[l],
                             dn_A_log[l], dn_dt_bias[l], dn_norm_w[l], w_out[l])
        h = _layer(h, params, fnw, final=(l == depth - 1))
    return h.reshape(seq, batch, d_model).transpose(1, 0, 2)
```

```python
import functools

import jax
import jax.numpy as jnp
from jax import lax
from jax.experimental import pallas as pl
from jax.experimental.pallas import tpu as pltpu

F32 = jnp.float32
BF16 = jnp.bfloat16

N_HEADS = 8
HEAD_DIM = 128
CONV_WIDTH = 4
LRU_C = 8.0
CHUNK = 64
EPS = 1e-6
SUBLANES = 8
LANES = 128
MXU_TILE = 256
N_CONV_COLS = 4 * HEAD_DIM
N_HEAD_COLS = 6 * HEAD_DIM
CARRY_ROWS = (CONV_WIDTH - 1) * SUBLANES
VMEM_LIMIT_BYTES = 60 * 1024 * 1024
HANDOVER = ("q", "k", "kb", "kbg", "qg", "kd", "vb", "gcb", "ylru", "zdn")
N_HANDOVER_SLOTS = 4


_sigmoid = jax.nn.sigmoid
_silu = jax.nn.silu
_softplus = jax.nn.softplus


def _dot(a, b):
    return jnp.dot(a, b, preferred_element_type=F32)


def _dot_nt(a, b):
    return lax.dot_general(a, b, (((1,), (1,)), ((), ())), preferred_element_type=F32)


def _dot_tn(a, b):
    return lax.dot_general(a, b, (((0,), (0,)), ((), ())), preferred_element_type=F32)


def _rms(x, w):
    return x * lax.rsqrt(jnp.mean(x * x, axis=-1, keepdims=True) + EPS) * w


def _interleave(*stage_generators):
    live = list(stage_generators)
    while live:
        for gen in list(live):
            try:
                next(gen)
            except StopIteration:
                live.remove(gen)


def _chain(*generator_fns):
    for fn in generator_fns:
        yield from fn()


def _layer_kernel(x_ref, xnext_ref, nw_ref, wh_ref, wba_ref, cw_ref, cb_ref, wg_ref, bg_ref,
                  lam_ref, lnw_ref, dnw_ref, alog_ref, dt_ref, wo_ref, fnw_ref,
                  o_ref,
                  xn_s, bgnext_s, bgcur_s, p_s, carry_s, hl_s, hlru_s, od0_s, od1_s, st_s,
                  pf0_s, pf1_s, *rest, in_bsd, out_bsd, final):
    rows = CHUNK * SUBLANES
    n_bufs = len(HANDOVER)
    rest = list(rest)
    ho_slots = [dict(zip(HANDOVER, [rest.pop(0) for _ in range(n_bufs)]))
                for _ in range(N_HANDOVER_SLOTS)]
    xres_s = rest.pop(0) if in_bsd else None
    acc_ref = rest.pop(0) if out_bsd else o_ref
    pf_slots = [pf0_s, pf1_s]
    od_s = [od0_s, od1_s]
    step = pl.program_id(0)
    d_model = xn_s.shape[1]

    def load_rows(src_ref):
        if not in_bsd:
            return src_ref[...]
        return jnp.swapaxes(src_ref[...], 0, 1).reshape(rows, d_model)

    lane = lax.broadcasted_iota(jnp.int32, (rows, LANES), 1)
    ii = lax.broadcasted_iota(jnp.int32, (CHUNK, CHUNK), 0)
    jj = lax.broadcasted_iota(jnp.int32, (CHUNK, CHUNK), 1)
    tri = ii >= jj
    strict = ii > jj

    def lane_bcast(vals, col):
        picked = jnp.sum(jnp.where(lane == col, vals, 0.0), axis=-1, keepdims=True)
        return jnp.broadcast_to(picked, vals.shape)

    def prepare(src_ref):
        x_rows = load_rows(src_ref)
        if in_bsd:
            xres_s[...] = x_rows
        xn_s[...] = _rms(x_rows, nw_ref[...]).astype(BF16)
        yield
        ba = _dot(xn_s[...], wba_ref[...])
        beta = _sigmoid(ba)
        g = -jnp.exp(alog_ref[...]) * _softplus(ba + dt_ref[...])
        acc = jnp.zeros((SUBLANES, LANES), F32)
        pieces = []
        for s in range(CHUNK):
            acc = acc + g[s * SUBLANES:(s + 1) * SUBLANES]
            pieces.append(acc)
        gc = jnp.concatenate(pieces, axis=0)
        bgnext_s[...] = jnp.where(lane < N_HEADS, beta, gc)
        yield

    def proj(h, pf_ref):
        for c in range(N_HEAD_COLS // MXU_TILE):
            cols = slice(c * MXU_TILE, (c + 1) * MXU_TILE)
            pf_ref[:, cols] = _dot(xn_s[...], wh_ref[h, :, cols])
            yield

    def front(h, pf_ref, bgsrc_ref, ho):
        pc = pf_ref[:, :N_CONV_COLS]
        p_s[0:CARRY_ROWS, :] = carry_s[h]
        p_s[CARRY_ROWS:CARRY_ROWS + rows, :] = pc
        carry_s[h] = pc[rows - CARRY_ROWS:rows, :]
        cwh = cw_ref[h]
        y = cwh[CONV_WIDTH - 1:CONV_WIDTH] * pc
        for j in range(CONV_WIDTH - 1):
            y = y + cwh[j:j + 1] * p_s[j * SUBLANES:j * SUBLANES + rows, :]
        xc = y[:, :HEAD_DIM] + cb_ref[h]
        qkv = _silu(y[:, HEAD_DIM:])
        yield

        gates = _dot(xc.astype(BF16), wg_ref[h]) + bg_ref[h]
        r = _sigmoid(gates[:, :HEAD_DIM])
        ig = _sigmoid(gates[:, HEAD_DIM:])
        log_a = r * (-LRU_C * _softplus(-lam_ref[h]))
        a = jnp.exp(log_a)
        mult = jnp.sqrt(jnp.tanh(-log_a) * (a * a + 1.0))
        bt = mult * (ig * xc)
        yield
        hprev = hlru_s[h]
        for s in range(CHUNK):
            sl = slice(s * SUBLANES, (s + 1) * SUBLANES)
            hprev = a[sl] * hprev + bt[sl]
            hl_s[sl, :] = hprev
        hlru_s[h] = hprev
        z_lru = pf_ref[:, N_CONV_COLS:N_CONV_COLS + HEAD_DIM]
        ho["ylru"][...] = _rms(hl_s[...], lnw_ref[h]) * _silu(z_lru)
        yield

        q = qkv[:, :HEAD_DIM]
        k = qkv[:, HEAD_DIM:2 * HEAD_DIM]
        v = qkv[:, 2 * HEAD_DIM:]
        q = q * lax.rsqrt(jnp.sum(q * q, axis=-1, keepdims=True) + EPS) * (HEAD_DIM ** -0.5)
        k = k * lax.rsqrt(jnp.sum(k * k, axis=-1, keepdims=True) + EPS)
        bgv = bgsrc_ref[...]
        betab = lane_bcast(bgv, h)
        gcb = lane_bcast(bgv, h + N_HEADS)
        yield
        eg = jnp.exp(gcb)
        glast = gcb[rows - SUBLANES:rows]
        kdf = jnp.exp(glast[None] - gcb.reshape(CHUNK, SUBLANES, LANES)).reshape(rows, LANES)
        kb = k * betab
        ho["q"][...] = q
        ho["k"][...] = k
        ho["kb"][...] = kb
        ho["kbg"][...] = kb * eg
        ho["qg"][...] = q * eg
        ho["kd"][...] = k * kdf
        ho["vb"][...] = v * betab
        ho["gcb"][...] = gcb
        ho["zdn"][...] = pf_ref[:, N_CONV_COLS + HEAD_DIM:]
        yield

    def back(h0, hos):
        n_hd = len(hos)
        cs = range(n_hd * SUBLANES)
        ho = [hos[c // SUBLANES] for c in cs]
        hd = [h0 + c // SUBLANES for c in cs]
        bt = [c % SUBLANES for c in cs]
        rsel = [pl.ds(bt[c], CHUNK, stride=SUBLANES) for c in cs]
        st = [st_s[hd[c], bt[c]] for c in cs]
        m1 = [_dot(jnp.concatenate([ho[c]["kbg"][rsel[c], :], ho[c]["qg"][rsel[c], :]],
                                   axis=0).astype(BF16), st[c].astype(BF16)) for c in cs]
        m2 = [_dot_nt(jnp.concatenate([ho[c]["kb"][rsel[c], :], ho[c]["q"][rsel[c], :]],
                                      axis=0).astype(BF16), ho[c]["k"][rsel[c], :].astype(BF16))
              for c in cs]
        yield
        dmask = []
        for c in cs:
            gcb_c = ho[c]["gcb"][rsel[c], :]
            dmask.append(jnp.exp(jnp.where(tri, gcb_c[:, :CHUNK] - gcb_c.T[:CHUNK, :], -jnp.inf)))
        pw = [jnp.where(strict, m2[c][:CHUNK] * dmask[c], 0.0).astype(BF16) for c in cs]
        attn = [(m2[c][CHUNK:] * dmask[c]).astype(BF16) for c in cs]
        xs = [ho[c]["vb"][rsel[c], :] - m1[c][:CHUNK] for c in cs]
        xs = [xs[c] - _dot(pw[c], xs[c].astype(BF16)) for c in cs]
        for _ in range(5):
            pw = [_dot(pw[c], pw[c]).astype(BF16) for c in cs]
            yield
            xs = [xs[c] + _dot(pw[c], xs[c].astype(BF16)) for c in cs]
        yield
        xsb = [xs[c].astype(BF16) for c in cs]
        for c in cs:
            od_s[c // SUBLANES][rsel[c], :] = m1[c][CHUNK:] + _dot(attn[c], xsb[c])
        for c in cs:
            last = rows - SUBLANES + bt[c]
            eglast = jnp.exp(ho[c]["gcb"][last:last + 1, :])
            st_s[hd[c], bt[c]] = (st[c] * eglast
                                  + _dot_tn(ho[c]["kd"][rsel[c], :].astype(BF16), xsb[c]))
        yield
        ys = []
        for i in range(n_hd):
            ys.append(hos[i]["ylru"][...])
            ys.append(_rms(od_s[i][...], dnw_ref[...]) * _silu(hos[i]["zdn"][...]))
        yh = jnp.concatenate(ys, axis=1).astype(BF16)
        wo = jnp.concatenate([wo_ref[h0 + i] for i in range(n_hd)], axis=0)
        acc_ref[...] += _dot(yh, wo)
        yield

    @pl.when(step == 0)
    def _():
        carry_s[...] = jnp.zeros_like(carry_s)
        hlru_s[...] = jnp.zeros_like(hlru_s)
        st_s[...] = jnp.zeros_like(st_s)
        _interleave(_chain(lambda: prepare(x_ref),
                           lambda: proj(0, pf_slots[0]),
                           lambda: proj(1, pf_slots[1]),
                           lambda: front(0, pf_slots[0], bgnext_s, ho_slots[0])))

    bgcur_s[...] = bgnext_s[...]
    acc_ref[...] = xres_s[...] if in_bsd else x_ref[...]

    n_ho = len(ho_slots)
    for i in range(N_HEADS):
        stages = []
        if i % 2 == 1:
            stages.append(back(i - 1, [ho_slots[(i - 1) % n_ho], ho_slots[i % n_ho]]))
        if i + 1 < N_HEADS:
            stages.append(front(i + 1, pf_slots[(i + 1) % 2], bgcur_s, ho_slots[(i + 1) % n_ho]))
        else:
            stages.append(front(0, pf_slots[0], bgnext_s, ho_slots[0]))
        if i + 2 < N_HEADS:
            stages.append(proj(i + 2, pf_slots[i % 2]))
        elif i + 2 == N_HEADS:
            stages.append(_chain(lambda: prepare(xnext_ref), lambda: proj(0, pf_slots[0])))
        else:
            stages.append(proj(1, pf_slots[1]))
        _interleave(*stages)

    if final or out_bsd:
        y = acc_ref[...]
        if final:
            y = _rms(y, fnw_ref[...])
        if out_bsd:
            y = jnp.swapaxes(y.reshape(CHUNK, SUBLANES, d_model), 0, 1)
        o_ref[...] = y


def _vmem_full():
    return pl.BlockSpec(memory_space=pltpu.VMEM)


def _layer(x, params, final_norm_w, *, batch, in_bsd, out_bsd, final):
    d_model = x.shape[-1]
    rows = CHUNK * SUBLANES
    n_steps = x.size // (rows * d_model)
    last = n_steps - 1

    def tile_spec(bsd, lookahead):
        pick = (lambda i: jnp.minimum(i + 1, last)) if lookahead else (lambda i: i)
        if bsd:
            return pl.BlockSpec((batch, CHUNK, d_model), lambda i: (0, pick(i), 0))
        return pl.BlockSpec((rows, d_model), lambda i: (pick(i), 0))

    out_shape = (batch, n_steps * CHUNK, d_model) if out_bsd else (n_steps * rows, d_model)
    slab = pltpu.VMEM((rows, LANES), F32)
    row_buf = pltpu.VMEM((rows, d_model), F32)
    pf_buf = pltpu.VMEM((rows, N_HEAD_COLS), F32)
    scratch = [
        pltpu.VMEM((rows, d_model), BF16),
        slab, slab,
        pltpu.VMEM((CARRY_ROWS + rows, N_CONV_COLS), F32),
        pltpu.VMEM((N_HEADS, CARRY_ROWS, N_CONV_COLS), F32),
        slab,
        pltpu.VMEM((N_HEADS, SUBLANES, HEAD_DIM), F32),
        slab, slab,
        pltpu.VMEM((N_HEADS, SUBLANES, HEAD_DIM, HEAD_DIM), F32),
        pf_buf, pf_buf,
    ] + [slab] * (N_HANDOVER_SLOTS * len(HANDOVER)) + [row_buf] * (int(in_bsd) + int(out_bsd))
    return pl.pallas_call(
        functools.partial(_layer_kernel, in_bsd=in_bsd, out_bsd=out_bsd, final=final),
        grid=(n_steps,),
        in_specs=[tile_spec(in_bsd, False), tile_spec(in_bsd, True)] + [_vmem_full()] * 14,
        out_specs=tile_spec(out_bsd, False),
        out_shape=jax.ShapeDtypeStruct(out_shape, F32),
        scratch_shapes=scratch,
        compiler_params=pltpu.CompilerParams(
            dimension_semantics=("arbitrary",), vmem_limit_bytes=VMEM_LIMIT_BYTES),
        name="hybrid_layer_final" if final else "hybrid_layer",
    )(x, x, *params, final_norm_w)


def _prep_layer(norm_w, w_in, lru_conv_w, lru_conv_b, lru_wa, lru_ba, lru_wx, lru_bx, lru_lambda,
                lru_norm_w, dn_conv_w, dn_A_log, dn_dt_bias, dn_norm_w, w_out):
    d_model = w_in.shape[0]
    w = HEAD_DIM * N_HEADS
    per_head = lambda t: t.reshape(t.shape[0], N_HEADS, HEAD_DIM)
    lx, lz, q, k, v, dz = (per_head(w_in[:, i * w:(i + 1) * w]) for i in range(6))
    wh = jnp.concatenate([lx, q, k, v, lz, dz], axis=2).transpose(1, 0, 2).astype(BF16)
    wba = jnp.zeros((d_model, LANES), F32).at[:, :2 * N_HEADS].set(w_in[:, 6 * w:]).astype(BF16)
    cq, ck, cv = (per_head(dn_conv_w[:, i * w:(i + 1) * w]) for i in range(3))
    cw = jnp.concatenate([per_head(lru_conv_w), cq, ck, cv], axis=2).transpose(1, 0, 2)
    row = lambda t: t.reshape(N_HEADS, 1, HEAD_DIM)
    wg = jnp.concatenate([lru_wa, lru_wx], axis=2).astype(BF16)
    bg = jnp.concatenate([row(lru_ba), row(lru_bx)], axis=2)
    alog = jnp.zeros((1, LANES), F32).at[0, N_HEADS:2 * N_HEADS].set(dn_A_log)
    dt = jnp.zeros((1, LANES), F32).at[0, N_HEADS:2 * N_HEADS].set(dn_dt_bias)
    wo = jnp.concatenate([w_out[:w].reshape(N_HEADS, HEAD_DIM, d_model),
                          w_out[w:].reshape(N_HEADS, HEAD_DIM, d_model)], axis=1).astype(BF16)
    return (norm_w.reshape(1, d_model), wh, wba, cw, row(lru_conv_b), wg, bg, row(lru_lambda),
            row(lru_norm_w), dn_norm_w.reshape(1, HEAD_DIM), alog, dt, wo)


def kernel(x, norm_w, w_in, lru_conv_w, lru_conv_b, lru_wa, lru_ba, lru_wx, lru_bx, lru_lambda,
           lru_norm_w, dn_conv_w, dn_A_log, dn_dt_bias, dn_norm_w, w_out, final_norm_w):
    batch, seq, d_model = x.shape
    depth = norm_w.shape[0]
    assert batch == SUBLANES and seq % CHUNK == 0
    assert w_in.shape[2] == 6 * N_HEADS * HEAD_DIM + 2 * N_HEADS
    h = x
    fnw = final_norm_w.reshape(1, d_model)
    for l in range(depth):
        params = _prep_layer(norm_w[l], w_in[l], lru_conv_w[l], lru_conv_b[l], lru_wa[l], lru_ba[l],
                             lru_wx[l], lru_bx[l], lru_lambda[l], lru_norm_w[l], dn_conv_w[l],
                             dn_A_log[l], dn_dt_bias[l], dn_norm_w[l], w_out[l])
        h = _layer(h, params, fnw, batch=batch, in_bsd=(l == 0), out_bsd=(l == depth - 1),
                   final=(l == depth - 1))
    return h
```

```python
import functools

import jax
import jax.numpy as jnp
from jax import lax
from jax.experimental import pallas as pl
from jax.experimental.pallas import tpu as pltpu

F32 = jnp.float32
BF16 = jnp.bfloat16

N_HEADS = 8
HEAD_DIM = 128
CONV_WIDTH = 4
LRU_C = 8.0
CHUNK = 64
EPS = 1e-6
SUBLANES = 8
LANES = 128
MXU_TILE = 256
N_CONV_COLS = 4 * HEAD_DIM
N_HEAD_COLS = 6 * HEAD_DIM
CARRY_ROWS = (CONV_WIDTH - 1) * SUBLANES
VMEM_LIMIT_BYTES = 60 * 1024 * 1024
HANDOVER = ("q", "k", "kb", "kbg", "qg", "kd", "vb", "gcb", "ylru", "zdn")
N_HANDOVER_SLOTS = 4


_sigmoid = jax.nn.sigmoid
_silu = jax.nn.silu
_softplus = jax.nn.softplus


def _dot(a, b):
    return jnp.dot(a, b, preferred_element_type=F32)


def _dot_nt(a, b):
    return lax.dot_general(a, b, (((1,), (1,)), ((), ())), preferred_element_type=F32)


def _dot_tn(a, b):
    return lax.dot_general(a, b, (((0,), (0,)), ((), ())), preferred_element_type=F32)


def _rms(x, w):
    return x * lax.rsqrt(jnp.mean(x * x, axis=-1, keepdims=True) + EPS) * w


def _interleave(*stage_generators):
    live = list(stage_generators)
    while live:
        for gen in list(live):
            try:
                next(gen)
            except StopIteration:
                live.remove(gen)


def _chain(*generator_fns):
    for fn in generator_fns:
        yield from fn()


def _layer_kernel(x_ref, xnext_ref, nw_ref, wh_ref, wba_ref, cw_ref, cb_ref, wg_ref, bg_ref,
                  lam_ref, lnw_ref, dnw_ref, alog_ref, dt_ref, wo_ref, fnw_ref,
                  o_ref,
                  xn_s, bgnext_s, bgcur_s, p_s, carry_s, hl_s, hlru_s, od0_s, od1_s, st_s,
                  pf0_s, pf1_s, *rest, in_bsd, out_bsd, final):
    rows = CHUNK * SUBLANES
    n_bufs = len(HANDOVER)
    rest = list(rest)
    ho_slots = [dict(zip(HANDOVER, [rest.pop(0) for _ in range(n_bufs)]))
                for _ in range(N_HANDOVER_SLOTS)]
    xres_s = rest.pop(0) if in_bsd else None
    acc_ref = rest.pop(0) if out_bsd else o_ref
    pf_slots = [pf0_s, pf1_s]
    od_s = [od0_s, od1_s]
    step = pl.program_id(0)
    d_model = xn_s.shape[1]

    def load_rows(src_ref):
        if not in_bsd:
            return src_ref[...]
        return jnp.swapaxes(src_ref[...], 0, 1).reshape(rows, d_model)

    lane = lax.broadcasted_iota(jnp.int32, (rows, LANES), 1)
    row2 = lax.broadcasted_iota(jnp.int32, (CHUNK, LANES), 0)
    lane2 = lax.broadcasted_iota(jnp.int32, (CHUNK, LANES), 1)
    left_half = lane2 < CHUNK
    col2 = jnp.bitwise_and(lane2, CHUNK - 1)
    tri2 = row2 >= col2
    strict2 = row2 > col2

    def lane_bcast(vals, col):
        picked = jnp.sum(jnp.where(lane == col, vals, 0.0), axis=-1, keepdims=True)
        return jnp.broadcast_to(picked, vals.shape)

    def prepare(src_ref):
        x_rows = load_rows(src_ref)
        if in_bsd:
            xres_s[...] = x_rows
        xn_s[...] = _rms(x_rows, nw_ref[...]).astype(BF16)
        yield
        ba = _dot(xn_s[...], wba_ref[...])
        beta = _sigmoid(ba)
        g = -jnp.exp(alog_ref[...]) * _softplus(ba + dt_ref[...])
        acc = jnp.zeros((SUBLANES, LANES), F32)
        pieces = []
        for s in range(CHUNK):
            acc = acc + g[s * SUBLANES:(s + 1) * SUBLANES]
            pieces.append(acc)
        gc = jnp.concatenate(pieces, axis=0)
        bgnext_s[...] = jnp.where(lane < N_HEADS, beta, gc)
        yield

    def proj(h, pf_ref):
        for c in range(N_HEAD_COLS // MXU_TILE):
            cols = slice(c * MXU_TILE, (c + 1) * MXU_TILE)
            pf_ref[:, cols] = _dot(xn_s[...], wh_ref[h, :, cols])
            yield

    def front(h, pf_ref, bgsrc_ref, ho):
        pc = pf_ref[:, :N_CONV_COLS]
        p_s[0:CARRY_ROWS, :] = carry_s[h]
        p_s[CARRY_ROWS:CARRY_ROWS + rows, :] = pc
        carry_s[h] = pc[rows - CARRY_ROWS:rows, :]
        cwh = cw_ref[h]
        y = cwh[CONV_WIDTH - 1:CONV_WIDTH] * pc
        for j in range(CONV_WIDTH - 1):
            y = y + cwh[j:j + 1] * p_s[j * SUBLANES:j * SUBLANES + rows, :]
        xc = y[:, :HEAD_DIM] + cb_ref[h]
        qkv = _silu(y[:, HEAD_DIM:])
        yield

        gates = _dot(xc.astype(BF16), wg_ref[h]) + bg_ref[h]
        r = _sigmoid(gates[:, :HEAD_DIM])
        ig = _sigmoid(gates[:, HEAD_DIM:])
        log_a = r * (-LRU_C * _softplus(-lam_ref[h]))
        a = jnp.exp(log_a)
        mult = jnp.sqrt(jnp.tanh(-log_a) * (a * a + 1.0))
        bt = mult * (ig * xc)
        yield
        hprev = hlru_s[h]
        for s in range(CHUNK):
            sl = slice(s * SUBLANES, (s + 1) * SUBLANES)
            hprev = a[sl] * hprev + bt[sl]
            hl_s[sl, :] = hprev
        hlru_s[h] = hprev
        z_lru = pf_ref[:, N_CONV_COLS:N_CONV_COLS + HEAD_DIM]
        ho["ylru"][...] = _rms(hl_s[...], lnw_ref[h]) * _silu(z_lru)
        yield

        q = qkv[:, :HEAD_DIM]
        k = qkv[:, HEAD_DIM:2 * HEAD_DIM]
        v = qkv[:, 2 * HEAD_DIM:]
        q = q * lax.rsqrt(jnp.sum(q * q, axis=-1, keepdims=True) + EPS) * (HEAD_DIM ** -0.5)
        k = k * lax.rsqrt(jnp.sum(k * k, axis=-1, keepdims=True) + EPS)
        bgv = bgsrc_ref[...]
        betab = lane_bcast(bgv, h)
        gcb = lane_bcast(bgv, h + N_HEADS)
        yield
        eg = jnp.exp(gcb)
        glast = gcb[rows - SUBLANES:rows]
        kdf = jnp.exp(glast[None] - gcb.reshape(CHUNK, SUBLANES, LANES)).reshape(rows, LANES)
        kb = k * betab
        ho["q"][...] = q
        ho["k"][...] = k
        ho["kb"][...] = kb
        ho["kbg"][...] = kb * eg
        ho["qg"][...] = q * eg
        ho["kd"][...] = k * kdf
        ho["vb"][...] = v * betab
        ho["gcb"][...] = gcb
        ho["zdn"][...] = pf_ref[:, N_CONV_COLS + HEAD_DIM:]
        yield

    def back(h0, hos):
        n_hd = len(hos)
        cs = range(n_hd * SUBLANES)
        ho = [hos[c // SUBLANES] for c in cs]
        hd = [h0 + c // SUBLANES for c in cs]
        bt = [c % SUBLANES for c in cs]
        rsel = [pl.ds(bt[c], CHUNK, stride=SUBLANES) for c in cs]
        st = [st_s[hd[c], bt[c]] for c in cs]
        m1 = [_dot(jnp.concatenate([ho[c]["kbg"][rsel[c], :], ho[c]["qg"][rsel[c], :]],
                                   axis=0).astype(BF16), st[c].astype(BF16)) for c in cs]
        ps = range(SUBLANES)
        zeros_c = jnp.zeros((CHUNK, LANES), F32)

        def for_problem(c, x):
            return jnp.concatenate([x, zeros_c] if c < SUBLANES else [zeros_c, x],
                                   axis=0).astype(BF16)

        m2 = []
        for p in ps:
            a, b2 = p, p + SUBLANES
            lhs = jnp.concatenate(
                [jnp.concatenate([ho[a]["kb"][rsel[a], :], ho[b2]["kb"][rsel[b2], :]], axis=1),
                 jnp.concatenate([ho[a]["q"][rsel[a], :], ho[b2]["q"][rsel[b2], :]], axis=1)],
                axis=0).astype(BF16)
            rhs = jnp.concatenate(
                [jnp.concatenate([ho[a]["k"][rsel[a], :], zeros_c], axis=1),
                 jnp.concatenate([zeros_c, ho[b2]["k"][rsel[b2], :]], axis=1)],
                axis=0).astype(BF16)
            m2.append(_dot_nt(lhs, rhs))
        yield
        pw, attn = [], []
        for p in ps:
            a, b2 = p, p + SUBLANES
            gcb_a = ho[a]["gcb"][rsel[a], :]
            gcb_b = ho[b2]["gcb"][rsel[b2], :]
            g_col = jnp.where(left_half, gcb_a, gcb_b)
            g_row = jnp.concatenate([gcb_a, gcb_b], axis=0).T[:CHUNK, :]
            dmask = jnp.exp(jnp.where(tri2, g_col - g_row, -jnp.inf))
            pw.append(jnp.where(strict2, m2[p][:CHUNK] * dmask, 0.0))
            attn.append((m2[p][CHUNK:] * dmask).astype(BF16))
        xs = [ho[c]["vb"][rsel[c], :] - m1[c][:CHUNK] for c in cs]
        pwb = [pw[p].astype(BF16) for p in ps]
        xs = [xs[c] - _dot(pwb[c % SUBLANES], for_problem(c, xs[c])) for c in cs]
        for _ in range(5):
            blockdiag = [jnp.concatenate([jnp.where(left_half, pw[p], 0.0),
                                          jnp.where(left_half, 0.0, pw[p])],
                                         axis=0).astype(BF16) for p in ps]
            pw = [_dot(pwb[p], blockdiag[p]) for p in ps]
            pwb = [pw[p].astype(BF16) for p in ps]
            yield
            xs = [xs[c] + _dot(pwb[c % SUBLANES], for_problem(c, xs[c])) for c in cs]
        yield
        xsb = [xs[c].astype(BF16) for c in cs]
        for c in cs:
            od_s[c // SUBLANES][rsel[c], :] = (m1[c][CHUNK:]
                                               + _dot(attn[c % SUBLANES], for_problem(c, xs[c])))
        for c in cs:
            last = rows - SUBLANES + bt[c]
            eglast = jnp.exp(ho[c]["gcb"][last:last + 1, :])
            st_s[hd[c], bt[c]] = (st[c] * eglast
                                  + _dot_tn(ho[c]["kd"][rsel[c], :].astype(BF16), xsb[c]))
        yield
        ys = []
        for i in range(n_hd):
            ys.append(hos[i]["ylru"][...])
            ys.append(_rms(od_s[i][...], dnw_ref[...]) * _silu(hos[i]["zdn"][...]))
        yh = jnp.concatenate(ys, axis=1).astype(BF16)
        wo = jnp.concatenate([wo_ref[h0 + i] for i in range(n_hd)], axis=0)
        acc_ref[...] += _dot(yh, wo)
        yield

    @pl.when(step == 0)
    def _():
        carry_s[...] = jnp.zeros_like(carry_s)
        hlru_s[...] = jnp.zeros_like(hlru_s)
        st_s[...] = jnp.zeros_like(st_s)
        _interleave(_chain(lambda: prepare(x_ref),
                           lambda: proj(0, pf_slots[0]),
                           lambda: proj(1, pf_slots[1]),
                           lambda: front(0, pf_slots[0], bgnext_s, ho_slots[0])))

    bgcur_s[...] = bgnext_s[...]
    acc_ref[...] = xres_s[...] if in_bsd else x_ref[...]

    n_ho = len(ho_slots)
    for i in range(N_HEADS):
        stages = []
        if i % 2 == 1:
            stages.append(back(i - 1, [ho_slots[(i - 1) % n_ho], ho_slots[i % n_ho]]))
        if i + 1 < N_HEADS:
            stages.append(front(i + 1, pf_slots[(i + 1) % 2], bgcur_s, ho_slots[(i + 1) % n_ho]))
        else:
            stages.append(front(0, pf_slots[0], bgnext_s, ho_slots[0]))
        if i + 2 < N_HEADS:
            stages.append(proj(i + 2, pf_slots[i % 2]))
        elif i + 2 == N_HEADS:
            stages.append(_chain(lambda: prepare(xnext_ref), lambda: proj(0, pf_slots[0])))
        else:
            stages.append(proj(1, pf_slots[1]))
        _interleave(*stages)

    if final or out_bsd:
        y = acc_ref[...]
        if final:
            y = _rms(y, fnw_ref[...])
        if out_bsd:
            y = jnp.swapaxes(y.reshape(CHUNK, SUBLANES, d_model), 0, 1)
        o_ref[...] = y


def _vmem_full():
    return pl.BlockSpec(memory_space=pltpu.VMEM)


def _layer(x, params, final_norm_w, *, batch, in_bsd, out_bsd, final):
    d_model = x.shape[-1]
    rows = CHUNK * SUBLANES
    n_steps = x.size // (rows * d_model)
    last = n_steps - 1

    def tile_spec(bsd, lookahead):
        pick = (lambda i: jnp.minimum(i + 1, last)) if lookahead else (lambda i: i)
        if bsd:
            return pl.BlockSpec((batch, CHUNK, d_model), lambda i: (0, pick(i), 0))
        return pl.BlockSpec((rows, d_model), lambda i: (pick(i), 0))

    out_shape = (batch, n_steps * CHUNK, d_model) if out_bsd else (n_steps * rows, d_model)
    slab = pltpu.VMEM((rows, LANES), F32)
    row_buf = pltpu.VMEM((rows, d_model), F32)
    pf_buf = pltpu.VMEM((rows, N_HEAD_COLS), F32)
    scratch = [
        pltpu.VMEM((rows, d_model), BF16),
        slab, slab,
        pltpu.VMEM((CARRY_ROWS + rows, N_CONV_COLS), F32),
        pltpu.VMEM((N_HEADS, CARRY_ROWS, N_CONV_COLS), F32),
        slab,
        pltpu.VMEM((N_HEADS, SUBLANES, HEAD_DIM), F32),
        slab, slab,
        pltpu.VMEM((N_HEADS, SUBLANES, HEAD_DIM, HEAD_DIM), F32),
        pf_buf, pf_buf,
    ] + [slab] * (N_HANDOVER_SLOTS * len(HANDOVER)) + [row_buf] * (int(in_bsd) + int(out_bsd))
    return pl.pallas_call(
        functools.partial(_layer_kernel, in_bsd=in_bsd, out_bsd=out_bsd, final=final),
        grid=(n_steps,),
        in_specs=[tile_spec(in_bsd, False), tile_spec(in_bsd, True)] + [_vmem_full()] * 14,
        out_specs=tile_spec(out_bsd, False),
        out_shape=jax.ShapeDtypeStruct(out_shape, F32),
        scratch_shapes=scratch,
        compiler_params=pltpu.CompilerParams(
            dimension_semantics=("arbitrary",), vmem_limit_bytes=VMEM_LIMIT_BYTES),
        name="hybrid_layer_final" if final else "hybrid_layer",
    )(x, x, *params, final_norm_w)


def _prep_layer(norm_w, w_in, lru_conv_w, lru_conv_b, lru_wa, lru_ba, lru_wx, lru_bx, lru_lambda,
                lru_norm_w, dn_conv_w, dn_A_log, dn_dt_bias, dn_norm_w, w_out):
    d_model = w_in.shape[0]
    w = HEAD_DIM * N_HEADS
    per_head = lambda t: t.reshape(t.shape[0], N_HEADS, HEAD_DIM)
    lx, lz, q, k, v, dz = (per_head(w_in[:, i * w:(i + 1) * w]) for i in range(6))
    wh = jnp.concatenate([lx, q, k, v, lz, dz], axis=2).transpose(1, 0, 2).astype(BF16)
    wba = jnp.zeros((d_model, LANES), F32).at[:, :2 * N_HEADS].set(w_in[:, 6 * w:]).astype(BF16)
    cq, ck, cv = (per_head(dn_conv_w[:, i * w:(i + 1) * w]) for i in range(3))
    cw = jnp.concatenate([per_head(lru_conv_w), cq, ck, cv], axis=2).transpose(1, 0, 2)
    row = lambda t: t.reshape(N_HEADS, 1, HEAD_DIM)
    wg = jnp.concatenate([lru_wa, lru_wx], axis=2).astype(BF16)
    bg = jnp.concatenate([row(lru_ba), row(lru_bx)], axis=2)
    alog = jnp.zeros((1, LANES), F32).at[0, N_HEADS:2 * N_HEADS].set(dn_A_log)
    dt = jnp.zeros((1, LANES), F32).at[0, N_HEADS:2 * N_HEADS].set(dn_dt_bias)
    wo = jnp.concatenate([w_out[:w].reshape(N_HEADS, HEAD_DIM, d_model),
                          w_out[w:].reshape(N_HEADS, HEAD_DIM, d_model)], axis=1).astype(BF16)
    return (norm_w.reshape(1, d_model), wh, wba, cw, row(lru_conv_b), wg, bg, row(lru_lambda),
            row(lru_norm_w), dn_norm_w.reshape(1, HEAD_DIM), alog, dt, wo)


def kernel(x, norm_w, w_in, lru_conv_w, lru_conv_b, lru_wa, lru_ba, lru_wx, lru_bx, lru_lambda,
           lru_norm_w, dn_conv_w, dn_A_log, dn_dt_bias, dn_norm_w, w_out, final_norm_w):
    batch, seq, d_model = x.shape
    depth = norm_w.shape[0]
    assert batch == SUBLANES and seq % CHUNK == 0
    assert w_in.shape[2] == 6 * N_HEADS * HEAD_DIM + 2 * N_HEADS
    h = x
    fnw = final_norm_w.reshape(1, d_model)
    for l in range(depth):
        params = _prep_layer(norm_w[l], w_in[l], lru_conv_w[l], lru_conv_b[l], lru_wa[l], lru_ba[l],
                             lru_wx[l], lru_bx[l], lru_lambda[l], lru_norm_w[l], dn_conv_w[l],
                             dn_A_log[l], dn_dt_bias[l], dn_norm_w[l], w_out[l])
        h = _layer(h, params, fnw, batch=batch, in_bsd=(l == 0), out_bsd=(l == depth - 1),
                   final=(l == depth - 1))
    return h
```

```python
import functools

import jax
import jax.numpy as jnp
from jax import lax
from jax.experimental import pallas as pl
from jax.experimental.pallas import tpu as pltpu

F32 = jnp.float32
BF16 = jnp.bfloat16

N_HEADS = 8
HEAD_DIM = 128
CONV_WIDTH = 4
LRU_C = 8.0
CHUNK = 64
EPS = 1e-6
SUBLANES = 8
LANES = 128
MXU_TILE = 256
N_CONV_COLS = 4 * HEAD_DIM
N_HEAD_COLS = 6 * HEAD_DIM
CARRY_ROWS = (CONV_WIDTH - 1) * SUBLANES
VMEM_LIMIT_BYTES = 60 * 1024 * 1024
HANDOVER_PER_BATCH = ("q", "k", "kb", "kbg", "qg", "kd", "vb", "gcb")
HANDOVER = HANDOVER_PER_BATCH + ("ylru", "zdn")
BATCH_PITCH = CHUNK + SUBLANES
N_HANDOVER_SLOTS = 4


_sigmoid = jax.nn.sigmoid
_silu = jax.nn.silu
_softplus = jax.nn.softplus


def _dot(a, b):
    return jnp.dot(a, b, preferred_element_type=F32)


def _dot_nt(a, b):
    return lax.dot_general(a, b, (((1,), (1,)), ((), ())), preferred_element_type=F32)


def _dot_tn(a, b):
    return lax.dot_general(a, b, (((0,), (0,)), ((), ())), preferred_element_type=F32)


def _rms(x, w):
    return x * lax.rsqrt(jnp.mean(x * x, axis=-1, keepdims=True) + EPS) * w


def _interleave(*stage_generators):
    live = list(stage_generators)
    while live:
        for gen in list(live):
            try:
                next(gen)
            except StopIteration:
                live.remove(gen)


def _chain(*generator_fns):
    for fn in generator_fns:
        yield from fn()


def _layer_kernel(x_ref, xnext_ref, nw_ref, wh_ref, wba_ref, cw_ref, cb_ref, wg_ref, bg_ref,
                  lam_ref, lnw_ref, dnw_ref, alog_ref, dt_ref, wo_ref, fnw_ref,
                  o_ref,
                  xn_s, bgnext_s, bgcur_s, p_s, carry_s, hl_s, hlru_s, od0_s, od1_s, st_s,
                  pf0_s, pf1_s, *rest, in_bsd, out_bsd, final):
    rows = CHUNK * SUBLANES
    n_bufs = len(HANDOVER)
    rest = list(rest)
    ho_slots = [dict(zip(HANDOVER, [rest.pop(0) for _ in range(n_bufs)]))
                for _ in range(N_HANDOVER_SLOTS)]
    xres_s = rest.pop(0) if in_bsd else None
    acc_ref = rest.pop(0) if out_bsd else o_ref
    pf_slots = [pf0_s, pf1_s]
    od_s = [od0_s, od1_s]
    step = pl.program_id(0)
    d_model = xn_s.shape[1]

    def load_rows(src_ref):
        if not in_bsd:
            return src_ref[...]
        return jnp.swapaxes(src_ref[...], 0, 1).reshape(rows, d_model)

    lane = lax.broadcasted_iota(jnp.int32, (rows, LANES), 1)
    row2 = lax.broadcasted_iota(jnp.int32, (CHUNK, LANES), 0)
    lane2 = lax.broadcasted_iota(jnp.int32, (CHUNK, LANES), 1)
    left_half = lane2 < CHUNK
    col2 = jnp.bitwise_and(lane2, CHUNK - 1)
    tri2 = row2 >= col2
    strict2 = row2 > col2

    def lane_bcast(vals, col):
        picked = jnp.sum(jnp.where(lane == col, vals, 0.0), axis=-1, keepdims=True)
        return jnp.broadcast_to(picked, vals.shape)

    def prepare(src_ref):
        x_rows = load_rows(src_ref)
        if in_bsd:
            xres_s[...] = x_rows
        xn_s[...] = _rms(x_rows, nw_ref[...]).astype(BF16)
        yield
        ba = _dot(xn_s[...], wba_ref[...])
        beta = _sigmoid(ba)
        g = -jnp.exp(alog_ref[...]) * _softplus(ba + dt_ref[...])
        acc = jnp.zeros((SUBLANES, LANES), F32)
        pieces = []
        for s in range(CHUNK):
            acc = acc + g[s * SUBLANES:(s + 1) * SUBLANES]
            pieces.append(acc)
        gc = jnp.concatenate(pieces, axis=0)
        bgnext_s[...] = jnp.where(lane < N_HEADS, beta, gc)
        yield

    def proj(h, pf_ref):
        for c in range(N_HEAD_COLS // MXU_TILE):
            cols = slice(c * MXU_TILE, (c + 1) * MXU_TILE)
            pf_ref[:, cols] = _dot(xn_s[...], wh_ref[h, :, cols])
            yield

    def front(h, pf_ref, bgsrc_ref, ho):
        pc = pf_ref[:, :N_CONV_COLS]
        p_s[0:CARRY_ROWS, :] = carry_s[h]
        p_s[CARRY_ROWS:CARRY_ROWS + rows, :] = pc
        carry_s[h] = pc[rows - CARRY_ROWS:rows, :]
        cwh = cw_ref[h]
        y = cwh[CONV_WIDTH - 1:CONV_WIDTH] * pc
        for j in range(CONV_WIDTH - 1):
            y = y + cwh[j:j + 1] * p_s[j * SUBLANES:j * SUBLANES + rows, :]
        xc = y[:, :HEAD_DIM] + cb_ref[h]
        qkv = _silu(y[:, HEAD_DIM:])
        yield

        gates = _dot(xc.astype(BF16), wg_ref[h]) + bg_ref[h]
        r = _sigmoid(gates[:, :HEAD_DIM])
        ig = _sigmoid(gates[:, HEAD_DIM:])
        log_a = r * (-LRU_C * _softplus(-lam_ref[h]))
        a = jnp.exp(log_a)
        mult = jnp.sqrt(jnp.tanh(-log_a) * (a * a + 1.0))
        bt = mult * (ig * xc)
        yield
        hprev = hlru_s[h]
        for s in range(CHUNK):
            sl = slice(s * SUBLANES, (s + 1) * SUBLANES)
            hprev = a[sl] * hprev + bt[sl]
            hl_s[sl, :] = hprev
        hlru_s[h] = hprev
        z_lru = pf_ref[:, N_CONV_COLS:N_CONV_COLS + HEAD_DIM]
        ho["ylru"][...] = _rms(hl_s[...], lnw_ref[h]) * _silu(z_lru)
        yield

        q = qkv[:, :HEAD_DIM]
        k = qkv[:, HEAD_DIM:2 * HEAD_DIM]
        v = qkv[:, 2 * HEAD_DIM:]
        q = q * lax.rsqrt(jnp.sum(q * q, axis=-1, keepdims=True) + EPS) * (HEAD_DIM ** -0.5)
        k = k * lax.rsqrt(jnp.sum(k * k, axis=-1, keepdims=True) + EPS)
        bgv = bgsrc_ref[...]
        betab = lane_bcast(bgv, h)
        gcb = lane_bcast(bgv, h + N_HEADS)
        yield
        eg = jnp.exp(gcb)
        glast = gcb[rows - SUBLANES:rows]
        kdf = jnp.exp(glast[None] - gcb.reshape(CHUNK, SUBLANES, LANES)).reshape(rows, LANES)
        kb = k * betab
        per_batch = {"q": q, "k": k, "kb": kb, "kbg": kb * eg, "qg": q * eg, "kd": k * kdf,
                     "vb": v * betab, "gcb": gcb}
        for name, val in per_batch.items():
            for s in range(CHUNK):
                ho[name][pl.ds(s, SUBLANES, stride=BATCH_PITCH), :] = (
                    val[s * SUBLANES:(s + 1) * SUBLANES])
        ho["zdn"][...] = pf_ref[:, N_CONV_COLS + HEAD_DIM:]
        yield

    def back(h0, hos):
        n_hd = len(hos)
        cs = range(n_hd * SUBLANES)
        ho = [hos[c // SUBLANES] for c in cs]
        hd = [h0 + c // SUBLANES for c in cs]
        bt = [c % SUBLANES for c in cs]
        rsel = [pl.ds(bt[c], CHUNK, stride=SUBLANES) for c in cs]
        blk = [pl.ds(bt[c] * BATCH_PITCH, CHUNK) for c in cs]
        st = [st_s[hd[c], bt[c]] for c in cs]
        m1 = [_dot(jnp.concatenate([ho[c]["kbg"][blk[c], :], ho[c]["qg"][blk[c], :]],
                                   axis=0).astype(BF16), st[c].astype(BF16)) for c in cs]
        ps = range(SUBLANES)
        zeros_c = jnp.zeros((CHUNK, LANES), F32)

        def for_problem(c, x):
            return jnp.concatenate([x, zeros_c] if c < SUBLANES else [zeros_c, x],
                                   axis=0).astype(BF16)

        m2 = []
        for p in ps:
            a, b2 = p, p + SUBLANES
            lhs = jnp.concatenate(
                [jnp.concatenate([ho[a]["kb"][blk[a], :], ho[b2]["kb"][blk[b2], :]], axis=1),
                 jnp.concatenate([ho[a]["q"][blk[a], :], ho[b2]["q"][blk[b2], :]], axis=1)],
                axis=0).astype(BF16)
            rhs = jnp.concatenate(
                [jnp.concatenate([ho[a]["k"][blk[a], :], zeros_c], axis=1),
                 jnp.concatenate([zeros_c, ho[b2]["k"][blk[b2], :]], axis=1)],
                axis=0).astype(BF16)
            m2.append(_dot_nt(lhs, rhs))
        yield
        pw, attn = [], []
        for p in ps:
            a, b2 = p, p + SUBLANES
            gcb_a = ho[a]["gcb"][blk[a], :]
            gcb_b = ho[b2]["gcb"][blk[b2], :]
            g_col = jnp.where(left_half, gcb_a, gcb_b)
            g_row = jnp.concatenate([gcb_a, gcb_b], axis=0).T[:CHUNK, :]
            dmask = jnp.exp(jnp.where(tri2, g_col - g_row, -jnp.inf))
            pw.append(jnp.where(strict2, m2[p][:CHUNK] * dmask, 0.0))
            attn.append((m2[p][CHUNK:] * dmask).astype(BF16))
        xs = [ho[c]["vb"][blk[c], :] - m1[c][:CHUNK] for c in cs]
        pwb = [pw[p].astype(BF16) for p in ps]
        xs = [xs[c] - _dot(pwb[c % SUBLANES], for_problem(c, xs[c])) for c in cs]
        for _ in range(5):
            blockdiag = [jnp.concatenate([jnp.where(left_half, pw[p], 0.0),
                                          jnp.where(left_half, 0.0, pw[p])],
                                         axis=0).astype(BF16) for p in ps]
            pw = [_dot(pwb[p], blockdiag[p]) for p in ps]
            pwb = [pw[p].astype(BF16) for p in ps]
            yield
            xs = [xs[c] + _dot(pwb[c % SUBLANES], for_problem(c, xs[c])) for c in cs]
        yield
        xsb = [xs[c].astype(BF16) for c in cs]
        for c in cs:
            od_s[c // SUBLANES][rsel[c], :] = (m1[c][CHUNK:]
                                               + _dot(attn[c % SUBLANES], for_problem(c, xs[c])))
        for c in cs:
            last = bt[c] * BATCH_PITCH + CHUNK - 1
            eglast = jnp.exp(ho[c]["gcb"][last:last + 1, :])
            st_s[hd[c], bt[c]] = (st[c] * eglast
                                  + _dot_tn(ho[c]["kd"][blk[c], :].astype(BF16), xsb[c]))
        yield
        ys = []
        for i in range(n_hd):
            ys.append(hos[i]["ylru"][...])
            ys.append(_rms(od_s[i][...], dnw_ref[...]) * _silu(hos[i]["zdn"][...]))
        yh = jnp.concatenate(ys, axis=1).astype(BF16)
        wo = jnp.concatenate([wo_ref[h0 + i] for i in range(n_hd)], axis=0)
        acc_ref[...] += _dot(yh, wo)
        yield

    @pl.when(step == 0)
    def _():
        carry_s[...] = jnp.zeros_like(carry_s)
        hlru_s[...] = jnp.zeros_like(hlru_s)
        st_s[...] = jnp.zeros_like(st_s)
        _interleave(_chain(lambda: prepare(x_ref),
                           lambda: proj(0, pf_slots[0]),
                           lambda: proj(1, pf_slots[1]),
                           lambda: front(0, pf_slots[0], bgnext_s, ho_slots[0])))

    bgcur_s[...] = bgnext_s[...]
    acc_ref[...] = xres_s[...] if in_bsd else x_ref[...]

    n_ho = len(ho_slots)
    for i in range(N_HEADS):
        stages = []
        if i % 2 == 1:
            stages.append(back(i - 1, [ho_slots[(i - 1) % n_ho], ho_slots[i % n_ho]]))
        if i + 1 < N_HEADS:
            stages.append(front(i + 1, pf_slots[(i + 1) % 2], bgcur_s, ho_slots[(i + 1) % n_ho]))
        else:
            stages.append(front(0, pf_slots[0], bgnext_s, ho_slots[0]))
        if i + 2 < N_HEADS:
            stages.append(proj(i + 2, pf_slots[i % 2]))
        elif i + 2 == N_HEADS:
            stages.append(_chain(lambda: prepare(xnext_ref), lambda: proj(0, pf_slots[0])))
        else:
            stages.append(proj(1, pf_slots[1]))
        _interleave(*stages)

    if final or out_bsd:
        y = acc_ref[...]
        if final:
            y = _rms(y, fnw_ref[...])
        if out_bsd:
            y = jnp.swapaxes(y.reshape(CHUNK, SUBLANES, d_model), 0, 1)
        o_ref[...] = y


def _vmem_full():
    return pl.BlockSpec(memory_space=pltpu.VMEM)


def _layer(x, params, final_norm_w, *, batch, in_bsd, out_bsd, final):
    d_model = x.shape[-1]
    rows = CHUNK * SUBLANES
    n_steps = x.size // (rows * d_model)
    last = n_steps - 1

    def tile_spec(bsd, lookahead):
        pick = (lambda i: jnp.minimum(i + 1, last)) if lookahead else (lambda i: i)
        if bsd:
            return pl.BlockSpec((batch, CHUNK, d_model), lambda i: (0, pick(i), 0))
        return pl.BlockSpec((rows, d_model), lambda i: (pick(i), 0))

    out_shape = (batch, n_steps * CHUNK, d_model) if out_bsd else (n_steps * rows, d_model)
    slab = pltpu.VMEM((rows, LANES), F32)
    batch_slab = pltpu.VMEM((SUBLANES * BATCH_PITCH, LANES), F32)
    row_buf = pltpu.VMEM((rows, d_model), F32)
    pf_buf = pltpu.VMEM((rows, N_HEAD_COLS), F32)
    scratch = [
        pltpu.VMEM((rows, d_model), BF16),
        slab, slab,
        pltpu.VMEM((CARRY_ROWS + rows, N_CONV_COLS), F32),
        pltpu.VMEM((N_HEADS, CARRY_ROWS, N_CONV_COLS), F32),
        slab,
        pltpu.VMEM((N_HEADS, SUBLANES, HEAD_DIM), F32),
        slab, slab,
        pltpu.VMEM((N_HEADS, SUBLANES, HEAD_DIM, HEAD_DIM), F32),
        pf_buf, pf_buf,
    ] + ([batch_slab] * len(HANDOVER_PER_BATCH)
         + [slab] * (len(HANDOVER) - len(HANDOVER_PER_BATCH))) * N_HANDOVER_SLOTS
    scratch += [row_buf] * (int(in_bsd) + int(out_bsd))
    return pl.pallas_call(
        functools.partial(_layer_kernel, in_bsd=in_bsd, out_bsd=out_bsd, final=final),
        grid=(n_steps,),
        in_specs=[tile_spec(in_bsd, False), tile_spec(in_bsd, True)] + [_vmem_full()] * 14,
        out_specs=tile_spec(out_bsd, False),
        out_shape=jax.ShapeDtypeStruct(out_shape, F32),
        scratch_shapes=scratch,
        compiler_params=pltpu.CompilerParams(
            dimension_semantics=("arbitrary",), vmem_limit_bytes=VMEM_LIMIT_BYTES),
        name="hybrid_layer_final" if final else "hybrid_layer",
    )(x, x, *params, final_norm_w)


def _prep_layer(norm_w, w_in, lru_conv_w, lru_conv_b, lru_wa, lru_ba, lru_wx, lru_bx, lru_lambda,
                lru_norm_w, dn_conv_w, dn_A_log, dn_dt_bias, dn_norm_w, w_out):
    d_model = w_in.shape[0]
    w = HEAD_DIM * N_HEADS
    per_head = lambda t: t.reshape(t.shape[0], N_HEADS, HEAD_DIM)
    lx, lz, q, k, v, dz = (per_head(w_in[:, i * w:(i + 1) * w]) for i in range(6))
    wh = jnp.concatenate([lx, q, k, v, lz, dz], axis=2).transpose(1, 0, 2).astype(BF16)
    wba = jnp.zeros((d_model, LANES), F32).at[:, :2 * N_HEADS].set(w_in[:, 6 * w:]).astype(BF16)
    cq, ck, cv = (per_head(dn_conv_w[:, i * w:(i + 1) * w]) for i in range(3))
    cw = jnp.concatenate([per_head(lru_conv_w), cq, ck, cv], axis=2).transpose(1, 0, 2)
    row = lambda t: t.reshape(N_HEADS, 1, HEAD_DIM)
    wg = jnp.concatenate([lru_wa, lru_wx], axis=2).astype(BF16)
    bg = jnp.concatenate([row(lru_ba), row(lru_bx)], axis=2)
    alog = jnp.zeros((1, LANES), F32).at[0, N_HEADS:2 * N_HEADS].set(dn_A_log)
    dt = jnp.zeros((1, LANES), F32).at[0, N_HEADS:2 * N_HEADS].set(dn_dt_bias)
    wo = jnp.concatenate([w_out[:w].reshape(N_HEADS, HEAD_DIM, d_model),
                          w_out[w:].reshape(N_HEADS, HEAD_DIM, d_model)], axis=1).astype(BF16)
    return (norm_w.reshape(1, d_model), wh, wba, cw, row(lru_conv_b), wg, bg, row(lru_lambda),
            row(lru_norm_w), dn_norm_w.reshape(1, HEAD_DIM), alog, dt, wo)


def kernel(x, norm_w, w_in, lru_conv_w, lru_conv_b, lru_wa, lru_ba, lru_wx, lru_bx, lru_lambda,
           lru_norm_w, dn_conv_w, dn_A_log, dn_dt_bias, dn_norm_w, w_out, final_norm_w):
    batch, seq, d_model = x.shape
    depth = norm_w.shape[0]
    assert batch == SUBLANES and seq % CHUNK == 0
    assert w_in.shape[2] == 6 * N_HEADS * HEAD_DIM + 2 * N_HEADS
    h = x
    fnw = final_norm_w.reshape(1, d_model)
    for l in range(depth):
        params = _prep_layer(norm_w[l], w_in[l], lru_conv_w[l], lru_conv_b[l], lru_wa[l], lru_ba[l],
                             lru_wx[l], lru_bx[l], lru_lambda[l], lru_norm_w[l], dn_conv_w[l],
                             dn_A_log[l], dn_dt_bias[l], dn_norm_w[l], w_out[l])
        h = _layer(h, params, fnw, batch=batch, in_bsd=(l == 0), out_bsd=(l == depth - 1),
                   final=(l == depth - 1))
    return h
```

```python
import functools

import jax
import jax.numpy as jnp
from jax import lax
from jax.experimental import pallas as pl
from jax.experimental.pallas import tpu as pltpu

F32 = jnp.float32
BF16 = jnp.bfloat16

N_HEADS = 8
HEAD_DIM = 128
CONV_WIDTH = 4
LRU_C = 8.0
CHUNK = 64
EPS = 1e-6
SUBLANES = 8
LANES = 128
MXU_TILE = 256
N_CONV_COLS = 4 * HEAD_DIM
N_HEAD_COLS = 6 * HEAD_DIM
CARRY_ROWS = (CONV_WIDTH - 1) * SUBLANES
VMEM_LIMIT_BYTES = 60 * 1024 * 1024
HANDOVER = ("q", "k", "kb", "kbg", "qg", "kd", "vb", "gcb", "ylru", "zdn")
N_HANDOVER_SLOTS = 4


_sigmoid = jax.nn.sigmoid
_silu = jax.nn.silu
_softplus = jax.nn.softplus


def _dot(a, b):
    return jnp.dot(a, b, preferred_element_type=F32)


def _dot_nt(a, b):
    return lax.dot_general(a, b, (((1,), (1,)), ((), ())), preferred_element_type=F32)


def _dot_tn(a, b):
    return lax.dot_general(a, b, (((0,), (0,)), ((), ())), preferred_element_type=F32)


def _rms(x, w):
    return x * lax.rsqrt(jnp.mean(x * x, axis=-1, keepdims=True) + EPS) * w


PREPARE_STAGES = 2
PROJ_STAGES = N_HEAD_COLS // MXU_TILE
FRONT_STAGES = 5
BACK_STAGES = 9


def _interleave(*staged):
    order = sorted(((i + 0.5) / n, k) for k, (_, n) in enumerate(staged) for i in range(n))
    for _, k in order:
        next(staged[k][0])
    exhausted = object()
    for gen, _ in staged:
        assert next(gen, exhausted) is exhausted, "stage count out of date"


def _chain(*generator_fns):
    for fn in generator_fns:
        yield from fn()


def _layer_kernel(x_ref, xnext_ref, nw_ref, wh_ref, wba_ref, cw_ref, cb_ref, wg_ref, bg_ref,
                  lam_ref, lnw_ref, dnw_ref, alog_ref, dt_ref, wo_ref, fnw_ref,
                  o_ref,
                  xn_s, bgnext_s, bgcur_s, p_s, carry_s, hl_s, hlru_s, od0_s, od1_s, st_s,
                  pf0_s, pf1_s, *rest, in_bsd, out_bsd, final):
    rows = CHUNK * SUBLANES
    n_bufs = len(HANDOVER)
    rest = list(rest)
    ho_slots = [dict(zip(HANDOVER, [rest.pop(0) for _ in range(n_bufs)]))
                for _ in range(N_HANDOVER_SLOTS)]
    xres_s = rest.pop(0) if in_bsd else None
    acc_ref = rest.pop(0) if out_bsd else o_ref
    pf_slots = [pf0_s, pf1_s]
    od_s = [od0_s, od1_s]
    step = pl.program_id(0)
    d_model = xn_s.shape[1]

    def load_rows(src_ref):
        if not in_bsd:
            return src_ref[...]
        return jnp.swapaxes(src_ref[...], 0, 1).reshape(rows, d_model)

    lane = lax.broadcasted_iota(jnp.int32, (rows, LANES), 1)
    row2 = lax.broadcasted_iota(jnp.int32, (CHUNK, LANES), 0)
    lane2 = lax.broadcasted_iota(jnp.int32, (CHUNK, LANES), 1)
    left_half = lane2 < CHUNK
    col2 = jnp.bitwise_and(lane2, CHUNK - 1)
    tri2 = row2 >= col2
    strict2 = row2 > col2

    def lane_bcast(vals, col):
        picked = jnp.sum(jnp.where(lane == col, vals, 0.0), axis=-1, keepdims=True)
        return jnp.broadcast_to(picked, vals.shape)

    def prepare(src_ref):
        x_rows = load_rows(src_ref)
        if in_bsd:
            xres_s[...] = x_rows
        xn_s[...] = _rms(x_rows, nw_ref[...]).astype(BF16)
        yield
        ba = _dot(xn_s[...], wba_ref[...])
        beta = _sigmoid(ba)
        g = -jnp.exp(alog_ref[...]) * _softplus(ba + dt_ref[...])
        acc = jnp.zeros((SUBLANES, LANES), F32)
        pieces = []
        for s in range(CHUNK):
            acc = acc + g[s * SUBLANES:(s + 1) * SUBLANES]
            pieces.append(acc)
        gc = jnp.concatenate(pieces, axis=0)
        bgnext_s[...] = jnp.where(lane < N_HEADS, beta, gc)
        yield

    def proj(h, pf_ref):
        for c in range(N_HEAD_COLS // MXU_TILE):
            cols = slice(c * MXU_TILE, (c + 1) * MXU_TILE)
            pf_ref[:, cols] = _dot(xn_s[...], wh_ref[h, :, cols])
            yield

    def front(h, pf_ref, bgsrc_ref, ho):
        pc = pf_ref[:, :N_CONV_COLS]
        p_s[0:CARRY_ROWS, :] = carry_s[h]
        p_s[CARRY_ROWS:CARRY_ROWS + rows, :] = pc
        carry_s[h] = pc[rows - CARRY_ROWS:rows, :]
        cwh = cw_ref[h]
        y = cwh[CONV_WIDTH - 1:CONV_WIDTH] * pc
        for j in range(CONV_WIDTH - 1):
            y = y + cwh[j:j + 1] * p_s[j * SUBLANES:j * SUBLANES + rows, :]
        xc = y[:, :HEAD_DIM] + cb_ref[h]
        qkv = _silu(y[:, HEAD_DIM:])
        yield

        gates = _dot(xc.astype(BF16), wg_ref[h]) + bg_ref[h]
        r = _sigmoid(gates[:, :HEAD_DIM])
        ig = _sigmoid(gates[:, HEAD_DIM:])
        log_a = r * (-LRU_C * _softplus(-lam_ref[h]))
        a = jnp.exp(log_a)
        mult = jnp.sqrt(jnp.tanh(-log_a) * (a * a + 1.0))
        bt = mult * (ig * xc)
        yield
        hprev = hlru_s[h]
        for s in range(CHUNK):
            sl = slice(s * SUBLANES, (s + 1) * SUBLANES)
            hprev = a[sl] * hprev + bt[sl]
            hl_s[sl, :] = hprev
        hlru_s[h] = hprev
        z_lru = pf_ref[:, N_CONV_COLS:N_CONV_COLS + HEAD_DIM]
        ho["ylru"][...] = _rms(hl_s[...], lnw_ref[h]) * _silu(z_lru)
        yield

        q = qkv[:, :HEAD_DIM]
        k = qkv[:, HEAD_DIM:2 * HEAD_DIM]
        v = qkv[:, 2 * HEAD_DIM:]
        q = q * lax.rsqrt(jnp.sum(q * q, axis=-1, keepdims=True) + EPS) * (HEAD_DIM ** -0.5)
        k = k * lax.rsqrt(jnp.sum(k * k, axis=-1, keepdims=True) + EPS)
        bgv = bgsrc_ref[...]
        betab = lane_bcast(bgv, h)
        gcb = lane_bcast(bgv, h + N_HEADS)
        yield
        eg = jnp.exp(gcb)
        glast = gcb[rows - SUBLANES:rows]
        kdf = jnp.exp(glast[None] - gcb.reshape(CHUNK, SUBLANES, LANES)).reshape(rows, LANES)
        kb = k * betab
        ho["q"][...] = q
        ho["k"][...] = k
        ho["kb"][...] = kb
        ho["kbg"][...] = kb * eg
        ho["qg"][...] = q * eg
        ho["kd"][...] = k * kdf
        ho["vb"][...] = v * betab
        ho["gcb"][...] = gcb
        ho["zdn"][...] = pf_ref[:, N_CONV_COLS + HEAD_DIM:]
        yield

    def back(h0, hos):
        n_hd = len(hos)
        cs = range(n_hd * SUBLANES)
        ho = [hos[c // SUBLANES] for c in cs]
        hd = [h0 + c // SUBLANES for c in cs]
        bt = [c % SUBLANES for c in cs]
        rsel = [pl.ds(bt[c], CHUNK, stride=SUBLANES) for c in cs]
        st = [st_s[hd[c], bt[c]] for c in cs]
        m1 = [_dot(jnp.concatenate([ho[c]["kbg"][rsel[c], :], ho[c]["qg"][rsel[c], :]],
                                   axis=0).astype(BF16), st[c].astype(BF16)) for c in cs]
        ps = range(SUBLANES)
        zeros_c = jnp.zeros((CHUNK, LANES), F32)

        def for_problem(c, x):
            return jnp.concatenate([x, zeros_c] if c < SUBLANES else [zeros_c, x],
                                   axis=0).astype(BF16)

        m2 = []
        for p in ps:
            a, b2 = p, p + SUBLANES
            lhs = jnp.concatenate(
                [jnp.concatenate([ho[a]["kb"][rsel[a], :], ho[b2]["kb"][rsel[b2], :]], axis=1),
                 jnp.concatenate([ho[a]["q"][rsel[a], :], ho[b2]["q"][rsel[b2], :]], axis=1)],
                axis=0).astype(BF16)
            rhs = jnp.concatenate(
                [jnp.concatenate([ho[a]["k"][rsel[a], :], zeros_c], axis=1),
                 jnp.concatenate([zeros_c, ho[b2]["k"][rsel[b2], :]], axis=1)],
                axis=0).astype(BF16)
            m2.append(_dot_nt(lhs, rhs))
        yield
        pw, attn = [], []
        for p in ps:
            a, b2 = p, p + SUBLANES
            gcb_a = ho[a]["gcb"][rsel[a], :]
            gcb_b = ho[b2]["gcb"][rsel[b2], :]
            g_col = jnp.where(left_half, gcb_a, gcb_b)
            g_row = jnp.concatenate([gcb_a, gcb_b], axis=0).T[:CHUNK, :]
            dmask = jnp.exp(jnp.where(tri2, g_col - g_row, -jnp.inf))
            pw.append(jnp.where(strict2, m2[p][:CHUNK] * dmask, 0.0))
            attn.append((m2[p][CHUNK:] * dmask).astype(BF16))
        xs = [ho[c]["vb"][rsel[c], :] - m1[c][:CHUNK] for c in cs]
        pwb = [pw[p].astype(BF16) for p in ps]
        xs = [xs[c] - _dot(pwb[c % SUBLANES], for_problem(c, xs[c])) for c in cs]
        for _ in range(5):
            blockdiag = [jnp.concatenate([jnp.where(left_half, pw[p], 0.0),
                                          jnp.where(left_half, 0.0, pw[p])],
                                         axis=0).astype(BF16) for p in ps]
            pw = [_dot(pwb[p], blockdiag[p]) for p in ps]
            pwb = [pw[p].astype(BF16) for p in ps]
            yield
            xs = [xs[c] + _dot(pwb[c % SUBLANES], for_problem(c, xs[c])) for c in cs]
        yield
        xsb = [xs[c].astype(BF16) for c in cs]
        for c in cs:
            od_s[c // SUBLANES][rsel[c], :] = (m1[c][CHUNK:]
                                               + _dot(attn[c % SUBLANES], for_problem(c, xs[c])))
        for c in cs:
            last = rows - SUBLANES + bt[c]
            eglast = jnp.exp(ho[c]["gcb"][last:last + 1, :])
            st_s[hd[c], bt[c]] = (st[c] * eglast
                                  + _dot_tn(ho[c]["kd"][rsel[c], :].astype(BF16), xsb[c]))
        yield
        ys = []
        for i in range(n_hd):
            ys.append(hos[i]["ylru"][...])
            ys.append(_rms(od_s[i][...], dnw_ref[...]) * _silu(hos[i]["zdn"][...]))
        yh = jnp.concatenate(ys, axis=1).astype(BF16)
        wo = jnp.concatenate([wo_ref[h0 + i] for i in range(n_hd)], axis=0)
        acc_ref[...] += _dot(yh, wo)
        yield

    @pl.when(step == 0)
    def _():
        carry_s[...] = jnp.zeros_like(carry_s)
        hlru_s[...] = jnp.zeros_like(hlru_s)
        st_s[...] = jnp.zeros_like(st_s)
        _interleave((_chain(lambda: prepare(x_ref),
                            lambda: proj(0, pf_slots[0]),
                            lambda: proj(1, pf_slots[1]),
                            lambda: front(0, pf_slots[0], bgnext_s, ho_slots[0])),
                     PREPARE_STAGES + 2 * PROJ_STAGES + FRONT_STAGES))

    bgcur_s[...] = bgnext_s[...]
    acc_ref[...] = xres_s[...] if in_bsd else x_ref[...]

    n_ho = len(ho_slots)
    for i in range(N_HEADS):
        stages = []
        if i % 2 == 1:
            stages.append((back(i - 1, [ho_slots[(i - 1) % n_ho], ho_slots[i % n_ho]]),
                           BACK_STAGES))
        if i + 1 < N_HEADS:
            stages.append((front(i + 1, pf_slots[(i + 1) % 2], bgcur_s,
                                 ho_slots[(i + 1) % n_ho]), FRONT_STAGES))
        else:
            stages.append((front(0, pf_slots[0], bgnext_s, ho_slots[0]), FRONT_STAGES))
        if i + 2 < N_HEADS:
            stages.append((proj(i + 2, pf_slots[i % 2]), PROJ_STAGES))
        elif i + 2 == N_HEADS:
            stages.append((_chain(lambda: prepare(xnext_ref), lambda: proj(0, pf_slots[0])),
                           PREPARE_STAGES + PROJ_STAGES))
        else:
            stages.append((proj(1, pf_slots[1]), PROJ_STAGES))
        _interleave(*stages)

    if final or out_bsd:
        y = acc_ref[...]
        if final:
            y = _rms(y, fnw_ref[...])
        if out_bsd:
            y = jnp.swapaxes(y.reshape(CHUNK, SUBLANES, d_model), 0, 1)
        o_ref[...] = y


def _vmem_full():
    return pl.BlockSpec(memory_space=pltpu.VMEM)


def _layer(x, params, final_norm_w, *, batch, in_bsd, out_bsd, final):
    d_model = x.shape[-1]
    rows = CHUNK * SUBLANES
    n_steps = x.size // (rows * d_model)
    last = n_steps - 1

    def tile_spec(bsd, lookahead):
        pick = (lambda i: jnp.minimum(i + 1, last)) if lookahead else (lambda i: i)
        if bsd:
            return pl.BlockSpec((batch, CHUNK, d_model), lambda i: (0, pick(i), 0))
        return pl.BlockSpec((rows, d_model), lambda i: (pick(i), 0))

    out_shape = (batch, n_steps * CHUNK, d_model) if out_bsd else (n_steps * rows, d_model)
    slab = pltpu.VMEM((rows, LANES), F32)
    row_buf = pltpu.VMEM((rows, d_model), F32)
    pf_buf = pltpu.VMEM((rows, N_HEAD_COLS), F32)
    scratch = [
        pltpu.VMEM((rows, d_model), BF16),
        slab, slab,
        pltpu.VMEM((CARRY_ROWS + rows, N_CONV_COLS), F32),
        pltpu.VMEM((N_HEADS, CARRY_ROWS, N_CONV_COLS), F32),
        slab,
        pltpu.VMEM((N_HEADS, SUBLANES, HEAD_DIM), F32),
        slab, slab,
        pltpu.VMEM((N_HEADS, SUBLANES, HEAD_DIM, HEAD_DIM), F32),
        pf_buf, pf_buf,
    ] + [slab] * (N_HANDOVER_SLOTS * len(HANDOVER)) + [row_buf] * (int(in_bsd) + int(out_bsd))
    return pl.pallas_call(
        functools.partial(_layer_kernel, in_bsd=in_bsd, out_bsd=out_bsd, final=final),
        grid=(n_steps,),
        in_specs=[tile_spec(in_bsd, False), tile_spec(in_bsd, True)] + [_vmem_full()] * 14,
        out_specs=tile_spec(out_bsd, False),
        out_shape=jax.ShapeDtypeStruct(out_shape, F32),
        scratch_shapes=scratch,
        compiler_params=pltpu.CompilerParams(
            dimension_semantics=("arbitrary",), vmem_limit_bytes=VMEM_LIMIT_BYTES),
        name="hybrid_layer_final" if final else "hybrid_layer",
    )(x, x, *params, final_norm_w)


def _prep_layer(norm_w, w_in, lru_conv_w, lru_conv_b, lru_wa, lru_ba, lru_wx, lru_bx, lru_lambda,
                lru_norm_w, dn_conv_w, dn_A_log, dn_dt_bias, dn_norm_w, w_out):
    d_model = w_in.shape[0]
    w = HEAD_DIM * N_HEADS
    per_head = lambda t: t.reshape(t.shape[0], N_HEADS, HEAD_DIM)
    lx, lz, q, k, v, dz = (per_head(w_in[:, i * w:(i + 1) * w]) for i in range(6))
    wh = jnp.concatenate([lx, q, k, v, lz, dz], axis=2).transpose(1, 0, 2).astype(BF16)
    wba = jnp.zeros((d_model, LANES), F32).at[:, :2 * N_HEADS].set(w_in[:, 6 * w:]).astype(BF16)
    cq, ck, cv = (per_head(dn_conv_w[:, i * w:(i + 1) * w]) for i in range(3))
    cw = jnp.concatenate([per_head(lru_conv_w), cq, ck, cv], axis=2).transpose(1, 0, 2)
    row = lambda t: t.reshape(N_HEADS, 1, HEAD_DIM)
    wg = jnp.concatenate([lru_wa, lru_wx], axis=2).astype(BF16)
    bg = jnp.concatenate([row(lru_ba), row(lru_bx)], axis=2)
    alog = jnp.zeros((1, LANES), F32).at[0, N_HEADS:2 * N_HEADS].set(dn_A_log)
    dt = jnp.zeros((1, LANES), F32).at[0, N_HEADS:2 * N_HEADS].set(dn_dt_bias)
    wo = jnp.concatenate([w_out[:w].reshape(N_HEADS, HEAD_DIM, d_model),
                          w_out[w:].reshape(N_HEADS, HEAD_DIM, d_model)], axis=1).astype(BF16)
    return (norm_w.reshape(1, d_model), wh, wba, cw, row(lru_conv_b), wg, bg, row(lru_lambda),
            row(lru_norm_w), dn_norm_w.reshape(1, HEAD_DIM), alog, dt, wo)


def kernel(x, norm_w, w_in, lru_conv_w, lru_conv_b, lru_wa, lru_ba, lru_wx, lru_bx, lru_lambda,
           lru_norm_w, dn_conv_w, dn_A_log, dn_dt_bias, dn_norm_w, w_out, final_norm_w):
    batch, seq, d_model = x.shape
    depth = norm_w.shape[0]
    assert batch == SUBLANES and seq % CHUNK == 0
    assert w_in.shape[2] == 6 * N_HEADS * HEAD_DIM + 2 * N_HEADS
    h = x
    fnw = final_norm_w.reshape(1, d_model)
    for l in range(depth):
        params = _prep_layer(norm_w[l], w_in[l], lru_conv_w[l], lru_conv_b[l], lru_wa[l], lru_ba[l],
                             lru_wx[l], lru_bx[l], lru_lambda[l], lru_norm_w[l], dn_conv_w[l],
                             dn_A_log[l], dn_dt_bias[l], dn_norm_w[l], w_out[l])
        h = _layer(h, params, fnw, batch=batch, in_bsd=(l == 0), out_bsd=(l == depth - 1),
                   final=(l == depth - 1))
    return h
```

```python
import functools

import jax
import jax.numpy as jnp
from jax import lax
from jax.experimental import pallas as pl
from jax.experimental.pallas import tpu as pltpu

F32 = jnp.float32
BF16 = jnp.bfloat16

N_HEADS = 8
HEAD_DIM = 128
CONV_WIDTH = 4
LRU_C = 8.0
CHUNK = 64
EPS = 1e-6
SUBLANES = 8
LANES = 128
MXU_TILE = 256
N_CONV_COLS = 4 * HEAD_DIM
N_HEAD_COLS = 6 * HEAD_DIM
CARRY_ROWS = (CONV_WIDTH - 1) * SUBLANES
VMEM_LIMIT_BYTES = 60 * 1024 * 1024
HANDOVER = ("q", "k", "kb", "kbg", "qg", "kd", "vb", "gcb", "ylru", "zdn")
N_HANDOVER_SLOTS = 4


_sigmoid = jax.nn.sigmoid
_silu = jax.nn.silu
_softplus = jax.nn.softplus


def _dot(a, b):
    return jnp.dot(a, b, preferred_element_type=F32)


def _dot_nt(a, b):
    return lax.dot_general(a, b, (((1,), (1,)), ((), ())), preferred_element_type=F32)


def _dot_tn(a, b):
    return lax.dot_general(a, b, (((0,), (0,)), ((), ())), preferred_element_type=F32)


def _rms(x, w):
    return x * lax.rsqrt(jnp.mean(x * x, axis=-1, keepdims=True) + EPS) * w


PREPARE_STAGES = 2
PROJ_STAGES = N_HEAD_COLS // MXU_TILE
FRONT_STAGES = 5
BACK_STAGES = 9


def _interleave(*staged):
    left = [n for _, n in staged]
    while any(left):
        for k, (gen, _) in enumerate(staged):
            for _ in range(min(left[k], 1 if k == 0 else 2)):
                next(gen)
                left[k] -= 1
    exhausted = object()
    for gen, _ in staged:
        assert next(gen, exhausted) is exhausted, "stage count out of date"


def _chain(*generator_fns):
    for fn in generator_fns:
        yield from fn()


def _layer_kernel(x_ref, xnext_ref, nw_ref, wh_ref, wba_ref, cw_ref, cb_ref, wg_ref, bg_ref,
                  lam_ref, lnw_ref, dnw_ref, alog_ref, dt_ref, wo_ref, fnw_ref,
                  o_ref,
                  xn_s, bgnext_s, bgcur_s, p_s, carry_s, hl_s, hlru_s, od0_s, od1_s, st_s,
                  pf0_s, pf1_s, *rest, in_bsd, out_bsd, final):
    rows = CHUNK * SUBLANES
    n_bufs = len(HANDOVER)
    rest = list(rest)
    ho_slots = [dict(zip(HANDOVER, [rest.pop(0) for _ in range(n_bufs)]))
                for _ in range(N_HANDOVER_SLOTS)]
    xres_s = rest.pop(0) if in_bsd else None
    acc_ref = rest.pop(0) if out_bsd else o_ref
    pf_slots = [pf0_s, pf1_s]
    od_s = [od0_s, od1_s]
    step = pl.program_id(0)
    d_model = xn_s.shape[1]

    def load_rows(src_ref):
        if not in_bsd:
            return src_ref[...]
        return jnp.swapaxes(src_ref[...], 0, 1).reshape(rows, d_model)

    lane = lax.broadcasted_iota(jnp.int32, (rows, LANES), 1)
    row2 = lax.broadcasted_iota(jnp.int32, (CHUNK, LANES), 0)
    lane2 = lax.broadcasted_iota(jnp.int32, (CHUNK, LANES), 1)
    left_half = lane2 < CHUNK
    col2 = jnp.bitwise_and(lane2, CHUNK - 1)
    tri2 = row2 >= col2
    strict2 = row2 > col2

    def lane_bcast(vals, col):
        picked = jnp.sum(jnp.where(lane == col, vals, 0.0), axis=-1, keepdims=True)
        return jnp.broadcast_to(picked, vals.shape)

    def prepare(src_ref):
        x_rows = load_rows(src_ref)
        if in_bsd:
            xres_s[...] = x_rows
        xn_s[...] = _rms(x_rows, nw_ref[...]).astype(BF16)
        yield
        ba = _dot(xn_s[...], wba_ref[...])
        beta = _sigmoid(ba)
        g = -jnp.exp(alog_ref[...]) * _softplus(ba + dt_ref[...])
        acc = jnp.zeros((SUBLANES, LANES), F32)
        pieces = []
        for s in range(CHUNK):
            acc = acc + g[s * SUBLANES:(s + 1) * SUBLANES]
            pieces.append(acc)
        gc = jnp.concatenate(pieces, axis=0)
        bgnext_s[...] = jnp.where(lane < N_HEADS, beta, gc)
        yield

    def proj(h, pf_ref):
        for c in range(N_HEAD_COLS // MXU_TILE):
            cols = slice(c * MXU_TILE, (c + 1) * MXU_TILE)
            pf_ref[:, cols] = _dot(xn_s[...], wh_ref[h, :, cols])
            yield

    def front(h, pf_ref, bgsrc_ref, ho):
        pc = pf_ref[:, :N_CONV_COLS]
        p_s[0:CARRY_ROWS, :] = carry_s[h]
        p_s[CARRY_ROWS:CARRY_ROWS + rows, :] = pc
        carry_s[h] = pc[rows - CARRY_ROWS:rows, :]
        cwh = cw_ref[h]
        y = cwh[CONV_WIDTH - 1:CONV_WIDTH] * pc
        for j in range(CONV_WIDTH - 1):
            y = y + cwh[j:j + 1] * p_s[j * SUBLANES:j * SUBLANES + rows, :]
        xc = y[:, :HEAD_DIM] + cb_ref[h]
        qkv = _silu(y[:, HEAD_DIM:])
        yield

        gates = _dot(xc.astype(BF16), wg_ref[h]) + bg_ref[h]
        r = _sigmoid(gates[:, :HEAD_DIM])
        ig = _sigmoid(gates[:, HEAD_DIM:])
        log_a = r * (-LRU_C * _softplus(-lam_ref[h]))
        a = jnp.exp(log_a)
        mult = jnp.sqrt(jnp.tanh(-log_a) * (a * a + 1.0))
        bt = mult * (ig * xc)
        yield
        hprev = hlru_s[h]
        for s in range(CHUNK):
            sl = slice(s * SUBLANES, (s + 1) * SUBLANES)
            hprev = a[sl] * hprev + bt[sl]
            hl_s[sl, :] = hprev
        hlru_s[h] = hprev
        z_lru = pf_ref[:, N_CONV_COLS:N_CONV_COLS + HEAD_DIM]
        ho["ylru"][...] = _rms(hl_s[...], lnw_ref[h]) * _silu(z_lru)
        yield

        q = qkv[:, :HEAD_DIM]
        k = qkv[:, HEAD_DIM:2 * HEAD_DIM]
        v = qkv[:, 2 * HEAD_DIM:]
        q = q * lax.rsqrt(jnp.sum(q * q, axis=-1, keepdims=True) + EPS) * (HEAD_DIM ** -0.5)
        k = k * lax.rsqrt(jnp.sum(k * k, axis=-1, keepdims=True) + EPS)
        bgv = bgsrc_ref[...]
        betab = lane_bcast(bgv, h)
        gcb = lane_bcast(bgv, h + N_HEADS)
        yield
        eg = jnp.exp(gcb)
        glast = gcb[rows - SUBLANES:rows]
        kdf = jnp.exp(glast[None] - gcb.reshape(CHUNK, SUBLANES, LANES)).reshape(rows, LANES)
        kb = k * betab
        ho["q"][...] = q
        ho["k"][...] = k
        ho["kb"][...] = kb
        ho["kbg"][...] = kb * eg
        ho["qg"][...] = q * eg
        ho["kd"][...] = k * kdf
        ho["vb"][...] = v * betab
        ho["gcb"][...] = gcb
        ho["zdn"][...] = pf_ref[:, N_CONV_COLS + HEAD_DIM:]
        yield

    def back(h0, hos):
        n_hd = len(hos)
        cs = range(n_hd * SUBLANES)
        ho = [hos[c // SUBLANES] for c in cs]
        hd = [h0 + c // SUBLANES for c in cs]
        bt = [c % SUBLANES for c in cs]
        rsel = [pl.ds(bt[c], CHUNK, stride=SUBLANES) for c in cs]
        st = [st_s[hd[c], bt[c]] for c in cs]
        m1 = [_dot(jnp.concatenate([ho[c]["kbg"][rsel[c], :], ho[c]["qg"][rsel[c], :]],
                                   axis=0).astype(BF16), st[c].astype(BF16)) for c in cs]
        ps = range(SUBLANES)
        zeros_c = jnp.zeros((CHUNK, LANES), F32)

        def for_problem(c, x):
            return jnp.concatenate([x, zeros_c] if c < SUBLANES else [zeros_c, x],
                                   axis=0).astype(BF16)

        m2 = []
        for p in ps:
            a, b2 = p, p + SUBLANES
            lhs = jnp.concatenate(
                [jnp.concatenate([ho[a]["kb"][rsel[a], :], ho[b2]["kb"][rsel[b2], :]], axis=1),
                 jnp.concatenate([ho[a]["q"][rsel[a], :], ho[b2]["q"][rsel[b2], :]], axis=1)],
                axis=0).astype(BF16)
            rhs = jnp.concatenate(
                [jnp.concatenate([ho[a]["k"][rsel[a], :], zeros_c], axis=1),
                 jnp.concatenate([zeros_c, ho[b2]["k"][rsel[b2], :]], axis=1)],
                axis=0).astype(BF16)
            m2.append(_dot_nt(lhs, rhs))
        yield
        pw, attn = [], []
        for p in ps:
            a, b2 = p, p + SUBLANES
            gcb_a = ho[a]["gcb"][rsel[a], :]
            gcb_b = ho[b2]["gcb"][rsel[b2], :]
            g_col = jnp.where(left_half, gcb_a, gcb_b)
            g_row = jnp.concatenate([gcb_a, gcb_b], axis=0).T[:CHUNK, :]
            dmask = jnp.exp(jnp.where(tri2, g_col - g_row, -jnp.inf))
            pw.append(jnp.where(strict2, m2[p][:CHUNK] * dmask, 0.0))
            attn.append((m2[p][CHUNK:] * dmask).astype(BF16))
        xs = [ho[c]["vb"][rsel[c], :] - m1[c][:CHUNK] for c in cs]
        pwb = [pw[p].astype(BF16) for p in ps]
        xs = [xs[c] - _dot(pwb[c % SUBLANES], for_problem(c, xs[c])) for c in cs]
        for _ in range(5):
            blockdiag = [jnp.concatenate([jnp.where(left_half, pw[p], 0.0),
                                          jnp.where(left_half, 0.0, pw[p])],
                                         axis=0).astype(BF16) for p in ps]
            pw = [_dot(pwb[p], blockdiag[p]) for p in ps]
            pwb = [pw[p].astype(BF16) for p in ps]
            yield
            xs = [xs[c] + _dot(pwb[c % SUBLANES], for_problem(c, xs[c])) for c in cs]
        yield
        xsb = [xs[c].astype(BF16) for c in cs]
        for c in cs:
            od_s[c // SUBLANES][rsel[c], :] = (m1[c][CHUNK:]
                                               + _dot(attn[c % SUBLANES], for_problem(c, xs[c])))
        for c in cs:
            last = rows - SUBLANES + bt[c]
            eglast = jnp.exp(ho[c]["gcb"][last:last + 1, :])
            st_s[hd[c], bt[c]] = (st[c] * eglast
                                  + _dot_tn(ho[c]["kd"][rsel[c], :].astype(BF16), xsb[c]))
        yield
        ys = []
        for i in range(n_hd):
            ys.append(hos[i]["ylru"][...])
            ys.append(_rms(od_s[i][...], dnw_ref[...]) * _silu(hos[i]["zdn"][...]))
        yh = jnp.concatenate(ys, axis=1).astype(BF16)
        wo = jnp.concatenate([wo_ref[h0 + i] for i in range(n_hd)], axis=0)
        acc_ref[...] += _dot(yh, wo)
        yield

    @pl.when(step == 0)
    def _():
        carry_s[...] = jnp.zeros_like(carry_s)
        hlru_s[...] = jnp.zeros_like(hlru_s)
        st_s[...] = jnp.zeros_like(st_s)
        _interleave((_chain(lambda: prepare(x_ref),
                            lambda: proj(0, pf_slots[0]),
                            lambda: proj(1, pf_slots[1]),
                            lambda: front(0, pf_slots[0], bgnext_s, ho_slots[0])),
                     PREPARE_STAGES + 2 * PROJ_STAGES + FRONT_STAGES))

    bgcur_s[...] = bgnext_s[...]
    acc_ref[...] = xres_s[...] if in_bsd else x_ref[...]

    n_ho = len(ho_slots)
    for i in range(N_HEADS):
        stages = []
        if i % 2 == 1:
            stages.append((back(i - 1, [ho_slots[(i - 1) % n_ho], ho_slots[i % n_ho]]),
                           BACK_STAGES))
        if i + 1 < N_HEADS:
            stages.append((front(i + 1, pf_slots[(i + 1) % 2], bgcur_s,
                                 ho_slots[(i + 1) % n_ho]), FRONT_STAGES))
        else:
            stages.append((front(0, pf_slots[0], bgnext_s, ho_slots[0]), FRONT_STAGES))
        if i + 2 < N_HEADS:
            stages.append((proj(i + 2, pf_slots[i % 2]), PROJ_STAGES))
        elif i + 2 == N_HEADS:
            stages.append((_chain(lambda: prepare(xnext_ref), lambda: proj(0, pf_slots[0])),
                           PREPARE_STAGES + PROJ_STAGES))
        else:
            stages.append((proj(1, pf_slots[1]), PROJ_STAGES))
        _interleave(*stages)

    if final or out_bsd:
        y = acc_ref[...]
        if final:
            y = _rms(y, fnw_ref[...])
        if out_bsd:
            y = jnp.swapaxes(y.reshape(CHUNK, SUBLANES, d_model), 0, 1)
        o_ref[...] = y


def _vmem_full():
    return pl.BlockSpec(memory_space=pltpu.VMEM)


def _layer(x, params, final_norm_w, *, batch, in_bsd, out_bsd, final):
    d_model = x.shape[-1]
    rows = CHUNK * SUBLANES
    n_steps = x.size // (rows * d_model)
    last = n_steps - 1

    def tile_spec(bsd, lookahead):
        pick = (lambda i: jnp.minimum(i + 1, last)) if lookahead else (lambda i: i)
        if bsd:
            return pl.BlockSpec((batch, CHUNK, d_model), lambda i: (0, pick(i), 0))
        return pl.BlockSpec((rows, d_model), lambda i: (pick(i), 0))

    out_shape = (batch, n_steps * CHUNK, d_model) if out_bsd else (n_steps * rows, d_model)
    slab = pltpu.VMEM((rows, LANES), F32)
    row_buf = pltpu.VMEM((rows, d_model), F32)
    pf_buf = pltpu.VMEM((rows, N_HEAD_COLS), F32)
    scratch = [
        pltpu.VMEM((rows, d_model), BF16),
        slab, slab,
        pltpu.VMEM((CARRY_ROWS + rows, N_CONV_COLS), F32),
        pltpu.VMEM((N_HEADS, CARRY_ROWS, N_CONV_COLS), F32),
        slab,
        pltpu.VMEM((N_HEADS, SUBLANES, HEAD_DIM), F32),
        slab, slab,
        pltpu.VMEM((N_HEADS, SUBLANES, HEAD_DIM, HEAD_DIM), F32),
        pf_buf, pf_buf,
    ] + [slab] * (N_HANDOVER_SLOTS * len(HANDOVER)) + [row_buf] * (int(in_bsd) + int(out_bsd))
    return pl.pallas_call(
        functools.partial(_layer_kernel, in_bsd=in_bsd, out_bsd=out_bsd, final=final),
        grid=(n_steps,),
        in_specs=[tile_spec(in_bsd, False), tile_spec(in_bsd, True)] + [_vmem_full()] * 14,
        out_specs=tile_spec(out_bsd, False),
        out_shape=jax.ShapeDtypeStruct(out_shape, F32),
        scratch_shapes=scratch,
        compiler_params=pltpu.CompilerParams(
            dimension_semantics=("arbitrary",), vmem_limit_bytes=VMEM_LIMIT_BYTES),
        name="hybrid_layer_final" if final else "hybrid_layer",
    )(x, x, *params, final_norm_w)


def _prep_layer(norm_w, w_in, lru_conv_w, lru_conv_b, lru_wa, lru_ba, lru_wx, lru_bx, lru_lambda,
                lru_norm_w, dn_conv_w, dn_A_log, dn_dt_bias, dn_norm_w, w_out):
    d_model = w_in.shape[0]
    w = HEAD_DIM * N_HEADS
    per_head = lambda t: t.reshape(t.shape[0], N_HEADS, HEAD_DIM)
    lx, lz, q, k, v, dz = (per_head(w_in[:, i * w:(i + 1) * w]) for i in range(6))
    wh = jnp.concatenate([lx, q, k, v, lz, dz], axis=2).transpose(1, 0, 2).astype(BF16)
    wba = jnp.zeros((d_model, LANES), F32).at[:, :2 * N_HEADS].set(w_in[:, 6 * w:]).astype(BF16)
    cq, ck, cv = (per_head(dn_conv_w[:, i * w:(i + 1) * w]) for i in range(3))
    cw = jnp.concatenate([per_head(lru_conv_w), cq, ck, cv], axis=2).transpose(1, 0, 2)
    row = lambda t: t.reshape(N_HEADS, 1, HEAD_DIM)
    wg = jnp.concatenate([lru_wa, lru_wx], axis=2).astype(BF16)
    bg = jnp.concatenate([row(lru_ba), row(lru_bx)], axis=2)
    alog = jnp.zeros((1, LANES), F32).at[0, N_HEADS:2 * N_HEADS].set(dn_A_log)
    dt = jnp.zeros((1, LANES), F32).at[0, N_HEADS:2 * N_HEADS].set(dn_dt_bias)
    wo = jnp.concatenate([w_out[:w].reshape(N_HEADS, HEAD_DIM, d_model),
                          w_out[w:].reshape(N_HEADS, HEAD_DIM, d_model)], axis=1).astype(BF16)
    return (norm_w.reshape(1, d_model), wh, wba, cw, row(lru_conv_b), wg, bg, row(lru_lambda),
            row(lru_norm_w), dn_norm_w.reshape(1, HEAD_DIM), alog, dt, wo)


def kernel(x, norm_w, w_in, lru_conv_w, lru_conv_b, lru_wa, lru_ba, lru_wx, lru_bx, lru_lambda,
           lru_norm_w, dn_conv_w, dn_A_log, dn_dt_bias, dn_norm_w, w_out, final_norm_w):
    batch, seq, d_model = x.shape
    depth = norm_w.shape[0]
    assert batch == SUBLANES and seq % CHUNK == 0
    assert w_in.shape[2] == 6 * N_HEADS * HEAD_DIM + 2 * N_HEADS
    h = x
    fnw = final_norm_w.reshape(1, d_model)
    for l in range(depth):
        params = _prep_layer(norm_w[l], w_in[l], lru_conv_w[l], lru_conv_b[l], lru_wa[l], lru_ba[l],
                             lru_wx[l], lru_bx[l], lru_lambda[l], lru_norm_w[l], dn_conv_w[l],
                             dn_A_log[l], dn_dt_bias[l], dn_norm_w[l], w_out[l])
        h = _layer(h, params, fnw, batch=batch, in_bsd=(l == 0), out_bsd=(l == depth - 1),
                   final=(l == depth - 1))
    return h
```

```python
import functools

import jax
import jax.numpy as jnp
from jax import lax
from jax.experimental import pallas as pl
from jax.experimental.pallas import tpu as pltpu

F32 = jnp.float32
BF16 = jnp.bfloat16

N_HEADS = 8
HEAD_DIM = 128
CONV_WIDTH = 4
LRU_C = 8.0
CHUNK = 64
EPS = 1e-6
SUBLANES = 8
LANES = 128
MXU_TILE = 256
N_CONV_COLS = 4 * HEAD_DIM
N_HEAD_COLS = 6 * HEAD_DIM
CARRY_ROWS = (CONV_WIDTH - 1) * SUBLANES
VMEM_LIMIT_BYTES = 60 * 1024 * 1024
HANDOVER = ("q", "k", "kb", "kbg", "qg", "kd", "vb", "gcb", "ylru", "zdn")
N_HANDOVER_SLOTS = 4


_sigmoid = jax.nn.sigmoid
_silu = jax.nn.silu
_softplus = jax.nn.softplus


def _dot(a, b):
    return jnp.dot(a, b, preferred_element_type=F32)


def _dot_nt(a, b):
    return lax.dot_general(a, b, (((1,), (1,)), ((), ())), preferred_element_type=F32)


def _dot_tn(a, b):
    return lax.dot_general(a, b, (((0,), (0,)), ((), ())), preferred_element_type=F32)


def _rms(x, w):
    n = x.shape[-1]
    return x * lax.rsqrt(jnp.sum(x * x, axis=-1, keepdims=True) + n * EPS) * (w * (n ** 0.5))


def _interleave(*stage_generators):
    live = list(stage_generators)
    while live:
        for gen in list(live):
            try:
                next(gen)
            except StopIteration:
                live.remove(gen)


def _chain(*generator_fns):
    for fn in generator_fns:
        yield from fn()


def _layer_kernel(x_ref, xnext_ref, nw_ref, wh_ref, wba_ref, cw_ref, cb_ref, wg_ref, bg_ref,
                  lam_ref, lnw_ref, dnw_ref, alog_ref, dt_ref, wo_ref, fnw_ref,
                  o_ref,
                  xn_s, bgnext_s, bgcur_s, p_s, carry_s, hl_s, hlru_s, od0_s, od1_s, st_s,
                  pf0_s, pf1_s, *rest, in_bsd, out_bsd, final):
    rows = CHUNK * SUBLANES
    n_bufs = len(HANDOVER)
    rest = list(rest)
    ho_slots = [dict(zip(HANDOVER, [rest.pop(0) for _ in range(n_bufs)]))
                for _ in range(N_HANDOVER_SLOTS)]
    xres_s = rest.pop(0) if in_bsd else None
    acc_ref = rest.pop(0) if out_bsd else o_ref
    pf_slots = [pf0_s, pf1_s]
    od_s = [od0_s, od1_s]
    step = pl.program_id(0)
    d_model = xn_s.shape[1]

    def load_rows(src_ref):
        if not in_bsd:
            return src_ref[...]
        return jnp.swapaxes(src_ref[...], 0, 1).reshape(rows, d_model)

    lane = lax.broadcasted_iota(jnp.int32, (rows, LANES), 1)
    row2 = lax.broadcasted_iota(jnp.int32, (CHUNK, LANES), 0)
    lane2 = lax.broadcasted_iota(jnp.int32, (CHUNK, LANES), 1)
    left_half = lane2 < CHUNK
    col2 = jnp.bitwise_and(lane2, CHUNK - 1)
    tri2 = row2 >= col2
    strict2 = row2 > col2

    def lane_bcast(vals, col):
        picked = jnp.sum(jnp.where(lane == col, vals, 0.0), axis=-1, keepdims=True)
        return jnp.broadcast_to(picked, vals.shape)

    def prepare(src_ref):
        x_rows = load_rows(src_ref)
        if in_bsd:
            xres_s[...] = x_rows
        xn_s[...] = _rms(x_rows, nw_ref[...]).astype(BF16)
        yield
        ba = _dot(xn_s[...], wba_ref[...])
        beta = _sigmoid(ba)
        g = -jnp.exp(alog_ref[...]) * _softplus(ba + dt_ref[...])
        acc = jnp.zeros((SUBLANES, LANES), F32)
        pieces = []
        for s in range(CHUNK):
            acc = acc + g[s * SUBLANES:(s + 1) * SUBLANES]
            pieces.append(acc)
        gc = jnp.concatenate(pieces, axis=0)
        bgnext_s[...] = jnp.where(lane < N_HEADS, beta, gc)
        yield

    def proj(h, pf_ref):
        for c in range(N_HEAD_COLS // MXU_TILE):
            cols = slice(c * MXU_TILE, (c + 1) * MXU_TILE)
            pf_ref[:, cols] = _dot(xn_s[...], wh_ref[h, :, cols])
            yield

    def front(h, pf_ref, bgsrc_ref, ho):
        pc = pf_ref[:, :N_CONV_COLS]
        p_s[0:CARRY_ROWS, :] = carry_s[h]
        p_s[CARRY_ROWS:CARRY_ROWS + rows, :] = pc
        carry_s[h] = pc[rows - CARRY_ROWS:rows, :]
        cwh = cw_ref[h]
        y = cwh[CONV_WIDTH - 1:CONV_WIDTH] * pc
        for j in range(CONV_WIDTH - 1):
            y = y + cwh[j:j + 1] * p_s[j * SUBLANES:j * SUBLANES + rows, :]
        xc = y[:, :HEAD_DIM] + cb_ref[h]
        qkv = _silu(y[:, HEAD_DIM:])
        yield

        gates = _dot(xc.astype(BF16), wg_ref[h]) + bg_ref[h]
        r = _sigmoid(gates[:, :HEAD_DIM])
        ig = _sigmoid(gates[:, HEAD_DIM:])
        log_a = r * (-LRU_C * _softplus(-lam_ref[h]))
        a = jnp.exp(log_a)
        mult = jnp.sqrt(jnp.tanh(-log_a) * (a * a + 1.0))
        bt = mult * (ig * xc)
        yield
        hprev = hlru_s[h]
        for s in range(CHUNK):
            sl = slice(s * SUBLANES, (s + 1) * SUBLANES)
            hprev = a[sl] * hprev + bt[sl]
            hl_s[sl, :] = hprev
        hlru_s[h] = hprev
        z_lru = pf_ref[:, N_CONV_COLS:N_CONV_COLS + HEAD_DIM]
        ho["ylru"][...] = _rms(hl_s[...], lnw_ref[h]) * _silu(z_lru)
        yield

        q = qkv[:, :HEAD_DIM]
        k = qkv[:, HEAD_DIM:2 * HEAD_DIM]
        v = qkv[:, 2 * HEAD_DIM:]
        q = q * lax.rsqrt(jnp.sum(q * q, axis=-1, keepdims=True) + EPS) * (HEAD_DIM ** -0.5)
        k = k * lax.rsqrt(jnp.sum(k * k, axis=-1, keepdims=True) + EPS)
        bgv = bgsrc_ref[...]
        betab = lane_bcast(bgv, h)
        gcb = lane_bcast(bgv, h + N_HEADS)
        yield
        eg = jnp.exp(gcb)
        glast = gcb[rows - SUBLANES:rows]
        kdf = jnp.exp(glast[None] - gcb.reshape(CHUNK, SUBLANES, LANES)).reshape(rows, LANES)
        kb = k * betab
        ho["q"][...] = q
        ho["k"][...] = k
        ho["kb"][...] = kb
        ho["kbg"][...] = kb * eg
        ho["qg"][...] = q * eg
        ho["kd"][...] = k * kdf
        ho["vb"][...] = v * betab
        ho["gcb"][...] = gcb
        ho["zdn"][...] = pf_ref[:, N_CONV_COLS + HEAD_DIM:]
        yield

    def back(h0, hos):
        n_hd = len(hos)
        cs = range(n_hd * SUBLANES)
        ho = [hos[c // SUBLANES] for c in cs]
        hd = [h0 + c // SUBLANES for c in cs]
        bt = [c % SUBLANES for c in cs]
        rsel = [pl.ds(bt[c], CHUNK, stride=SUBLANES) for c in cs]
        st = [st_s[hd[c], bt[c]] for c in cs]
        m1 = [_dot(jnp.concatenate([ho[c]["kbg"][rsel[c], :], ho[c]["qg"][rsel[c], :]],
                                   axis=0).astype(BF16), st[c].astype(BF16)) for c in cs]
        ps = range(SUBLANES)
        zeros_c = jnp.zeros((CHUNK, LANES), F32)

        def for_problem(c, x):
            return jnp.concatenate([x, zeros_c] if c < SUBLANES else [zeros_c, x],
                                   axis=0).astype(BF16)

        m2 = []
        for p in ps:
            a, b2 = p, p + SUBLANES
            lhs = jnp.concatenate(
                [jnp.concatenate([ho[a]["kb"][rsel[a], :], ho[b2]["kb"][rsel[b2], :]], axis=1),
                 jnp.concatenate([ho[a]["q"][rsel[a], :], ho[b2]["q"][rsel[b2], :]], axis=1)],
                axis=0).astype(BF16)
            rhs = jnp.concatenate(
                [jnp.concatenate([ho[a]["k"][rsel[a], :], zeros_c], axis=1),
                 jnp.concatenate([zeros_c, ho[b2]["k"][rsel[b2], :]], axis=1)],
                axis=0).astype(BF16)
            m2.append(_dot_nt(lhs, rhs))
        yield
        pw, attn = [], []
        for p in ps:
            a, b2 = p, p + SUBLANES
            gcb_a = ho[a]["gcb"][rsel[a], :]
            gcb_b = ho[b2]["gcb"][rsel[b2], :]
            g_col = jnp.where(left_half, gcb_a, gcb_b)
            g_row = jnp.concatenate([gcb_a, gcb_b], axis=0).T[:CHUNK, :]
            dmask = jnp.exp(jnp.where(tri2, g_col - g_row, -jnp.inf))
            pw.append(jnp.where(strict2, m2[p][:CHUNK] * dmask, 0.0))
            attn.append((m2[p][CHUNK:] * dmask).astype(BF16))
        xs = [ho[c]["vb"][rsel[c], :] - m1[c][:CHUNK] for c in cs]
        pwb = [pw[p].astype(BF16) for p in ps]
        xs = [xs[c] - _dot(pwb[c % SUBLANES], for_problem(c, xs[c])) for c in cs]
        for _ in range(5):
            blockdiag = [jnp.concatenate([jnp.where(left_half, pw[p], 0.0),
                                          jnp.where(left_half, 0.0, pw[p])],
                                         axis=0).astype(BF16) for p in ps]
            pw = [_dot(pwb[p], blockdiag[p]) for p in ps]
            pwb = [pw[p].astype(BF16) for p in ps]
            yield
            xs = [xs[c] + _dot(pwb[c % SUBLANES], for_problem(c, xs[c])) for c in cs]
        yield
        xsb = [xs[c].astype(BF16) for c in cs]
        for c in cs:
            od_s[c // SUBLANES][rsel[c], :] = (m1[c][CHUNK:]
                                               + _dot(attn[c % SUBLANES], for_problem(c, xs[c])))
        for c in cs:
            last = rows - SUBLANES + bt[c]
            eglast = jnp.exp(ho[c]["gcb"][last:last + 1, :])
            st_s[hd[c], bt[c]] = (st[c] * eglast
                                  + _dot_tn(ho[c]["kd"][rsel[c], :].astype(BF16), xsb[c]))
        yield
        ys = []
        for i in range(n_hd):
            ys.append(hos[i]["ylru"][...])
            ys.append(_rms(od_s[i][...], dnw_ref[...]) * _silu(hos[i]["zdn"][...]))
        yh = jnp.concatenate(ys, axis=1).astype(BF16)
        wo = jnp.concatenate([wo_ref[h0 + i] for i in range(n_hd)], axis=0)
        prev = (xres_s if in_bsd else x_ref) if h0 == 0 else acc_ref
        acc_ref[...] = prev[...] + _dot(yh, wo)
        yield

    @pl.when(step == 0)
    def _():
        carry_s[...] = jnp.zeros_like(carry_s)
        hlru_s[...] = jnp.zeros_like(hlru_s)
        st_s[...] = jnp.zeros_like(st_s)
        _interleave(_chain(lambda: prepare(x_ref),
                           lambda: proj(0, pf_slots[0]),
                           lambda: proj(1, pf_slots[1]),
                           lambda: front(0, pf_slots[0], bgnext_s, ho_slots[0])))

    bgcur_s[...] = bgnext_s[...]

    n_ho = len(ho_slots)
    for i in range(N_HEADS):
        stages = []
        if i % 2 == 1:
            stages.append(back(i - 1, [ho_slots[(i - 1) % n_ho], ho_slots[i % n_ho]]))
        if i + 1 < N_HEADS:
            stages.append(front(i + 1, pf_slots[(i + 1) % 2], bgcur_s, ho_slots[(i + 1) % n_ho]))
        else:
            stages.append(front(0, pf_slots[0], bgnext_s, ho_slots[0]))
        if i + 2 < N_HEADS:
            stages.append(proj(i + 2, pf_slots[i % 2]))
        elif i + 2 == N_HEADS:
            stages.append(_chain(lambda: prepare(xnext_ref), lambda: proj(0, pf_slots[0])))
        else:
            stages.append(proj(1, pf_slots[1]))
        _interleave(*stages)

    if final or out_bsd:
        y = acc_ref[...]
        if final:
            y = _rms(y, fnw_ref[...])
        if out_bsd:
            y = jnp.swapaxes(y.reshape(CHUNK, SUBLANES, d_model), 0, 1)
        o_ref[...] = y


def _vmem_full():
    return pl.BlockSpec(memory_space=pltpu.VMEM)


def _layer(x, params, final_norm_w, *, batch, in_bsd, out_bsd, final):
    d_model = x.shape[-1]
    rows = CHUNK * SUBLANES
    n_steps = x.size // (rows * d_model)
    last = n_steps - 1

    def tile_spec(bsd, lookahead):
        pick = (lambda i: jnp.minimum(i + 1, last)) if lookahead else (lambda i: i)
        if bsd:
            return pl.BlockSpec((batch, CHUNK, d_model), lambda i: (0, pick(i), 0))
        return pl.BlockSpec((rows, d_model), lambda i: (pick(i), 0))

    out_shape = (batch, n_steps * CHUNK, d_model) if out_bsd else (n_steps * rows, d_model)
    slab = pltpu.VMEM((rows, LANES), F32)
    row_buf = pltpu.VMEM((rows, d_model), F32)
    pf_buf = pltpu.VMEM((rows, N_HEAD_COLS), F32)
    scratch = [
        pltpu.VMEM((rows, d_model), BF16),
        slab, slab,
        pltpu.VMEM((CARRY_ROWS + rows, N_CONV_COLS), F32),
        pltpu.VMEM((N_HEADS, CARRY_ROWS, N_CONV_COLS), F32),
        slab,
        pltpu.VMEM((N_HEADS, SUBLANES, HEAD_DIM), F32),
        slab, slab,
        pltpu.VMEM((N_HEADS, SUBLANES, HEAD_DIM, HEAD_DIM), F32),
        pf_buf, pf_buf,
    ] + [slab] * (N_HANDOVER_SLOTS * len(HANDOVER)) + [row_buf] * (int(in_bsd) + int(out_bsd))
    return pl.pallas_call(
        functools.partial(_layer_kernel, in_bsd=in_bsd, out_bsd=out_bsd, final=final),
        grid=(n_steps,),
        in_specs=[tile_spec(in_bsd, False), tile_spec(in_bsd, True)] + [_vmem_full()] * 14,
        out_specs=tile_spec(out_bsd, False),
        out_shape=jax.ShapeDtypeStruct(out_shape, F32),
        scratch_shapes=scratch,
        compiler_params=pltpu.CompilerParams(
            dimension_semantics=("arbitrary",), vmem_limit_bytes=VMEM_LIMIT_BYTES),
        name="hybrid_layer_final" if final else "hybrid_layer",
    )(x, x, *params, final_norm_w)


def _prep_layer(norm_w, w_in, lru_conv_w, lru_conv_b, lru_wa, lru_ba, lru_wx, lru_bx, lru_lambda,
                lru_norm_w, dn_conv_w, dn_A_log, dn_dt_bias, dn_norm_w, w_out):
    d_model = w_in.shape[0]
    w = HEAD_DIM * N_HEADS
    per_head = lambda t: t.reshape(t.shape[0], N_HEADS, HEAD_DIM)
    lx, lz, q, k, v, dz = (per_head(w_in[:, i * w:(i + 1) * w]) for i in range(6))
    wh = jnp.concatenate([lx, q, k, v, lz, dz], axis=2).transpose(1, 0, 2).astype(BF16)
    wba = jnp.zeros((d_model, LANES), F32).at[:, :2 * N_HEADS].set(w_in[:, 6 * w:]).astype(BF16)
    cq, ck, cv = (per_head(dn_conv_w[:, i * w:(i + 1) * w]) for i in range(3))
    cw = jnp.concatenate([per_head(lru_conv_w), cq, ck, cv], axis=2).transpose(1, 0, 2)
    row = lambda t: t.reshape(N_HEADS, 1, HEAD_DIM)
    wg = jnp.concatenate([lru_wa, lru_wx], axis=2).astype(BF16)
    bg = jnp.concatenate([row(lru_ba), row(lru_bx)], axis=2)
    alog = jnp.zeros((1, LANES), F32).at[0, N_HEADS:2 * N_HEADS].set(dn_A_log)
    dt = jnp.zeros((1, LANES), F32).at[0, N_HEADS:2 * N_HEADS].set(dn_dt_bias)
    wo = jnp.concatenate([w_out[:w].reshape(N_HEADS, HEAD_DIM, d_model),
                          w_out[w:].reshape(N_HEADS, HEAD_DIM, d_model)], axis=1).astype(BF16)
    return (norm_w.reshape(1, d_model), wh, wba, cw, row(lru_conv_b), wg, bg, row(lru_lambda),
            row(lru_norm_w), dn_norm_w.reshape(1, HEAD_DIM), alog, dt, wo)


def kernel(x, norm_w, w_in, lru_conv_w, lru_conv_b, lru_wa, lru_ba, lru_wx, lru_bx, lru_lambda,
           lru_norm_w, dn_conv_w, dn_A_log, dn_dt_bias, dn_norm_w, w_out, final_norm_w):
    batch, seq, d_model = x.shape
    depth = norm_w.shape[0]
    assert batch == SUBLANES and seq % CHUNK == 0
    assert w_in.shape[2] == 6 * N_HEADS * HEAD_DIM + 2 * N_HEADS
    h = x
    fnw = final_norm_w.reshape(1, d_model)
    for l in range(depth):
        params = _prep_layer(norm_w[l], w_in[l], lru_conv_w[l], lru_conv_b[l], lru_wa[l], lru_ba[l],
                             lru_wx[l], lru_bx[l], lru_lambda[l], lru_norm_w[l], dn_conv_w[l],
                             dn_A_log[l], dn_dt_bias[l], dn_norm_w[l], w_out[l])
        h = _layer(h, params, fnw, batch=batch, in_bsd=(l == 0), out_bsd=(l == depth - 1),
                   final=(l == depth - 1))
    return h
```
